```python
import math
import jax
import jax.numpy as jnp
from jax import lax
import numpy as np


D_MODEL = 1024
BATCH = 1
SEQ = 16384
DEPTH = 4

GRID_W = 64
CTX_LEN = 256
N_MOD = 6

GLA_HEADS = 4
GLA_DK = 64
GLA_DV = 128
GLA_K = GLA_HEADS * GLA_DK
GLA_V = GLA_HEADS * GLA_DV
GLA_GATE_RANK = 16
GLA_TAU = 16.0
GLA_CHUNK = 64

RWKV_HEADS = 8
RWKV_N = 64
RWKV_W = RWKV_HEADS * RWKV_N
RWKV_DECAY_RANK = 64
RWKV_A_RANK = 64
RWKV_GATE_RANK = 128
RWKV_GN_EPS = 64e-5
RWKV_SIZES = (RWKV_W, RWKV_W, RWKV_W, RWKV_DECAY_RANK, RWKV_DECAY_RANK, RWKV_A_RANK, RWKV_A_RANK, RWKV_GATE_RANK)
RWKV_IN = sum(RWKV_SIZES)

S5_WIDTH = 512
S5_GROUP = 16
S5_GROUPS = S5_WIDTH // S5_GROUP
S5_STATE = 64

N_BRANCH = 3
BRANCH_W = 512
IN_SIZES = (GLA_K, GLA_K, GLA_V, GLA_V, GLA_GATE_RANK, GLA_GATE_RANK, RWKV_IN, S5_WIDTH, N_BRANCH * D_MODEL)
D_IN = sum(IN_SIZES)

PEER_HEADS = 8
PEER_NKEYS = 128
PEER_EXPERTS = PEER_NKEYS * PEER_NKEYS
PEER_DQ = 256
PEER_HALF = PEER_DQ // 2
PEER_TOPK = 16
PEER_BLOCK = 128

kernel_name = 'hybrid_gla_rwkv7_s5_peer_dit'


def rms_norm(x, g, eps=1e-6):
    xf = x.astype(jnp.float32)
    y = xf * lax.rsqrt(jnp.mean(xf * xf, axis=-1, keepdims=True) + eps)
    return (y * g.astype(jnp.float32)).astype(x.dtype)


def modulate(x, g, shift, scale):
    return rms_norm(x, g) * (1.0 + scale) + shift


def split_last(z, sizes):
    return jnp.split(z, [int(s) for s in np.cumsum(sizes)[:-1]], axis=-1)


def heads(t, n_heads):
    return t.reshape(t.shape[0], t.shape[1], n_heads, -1).astype(jnp.float32)


def flip_seq(t):
    return jnp.flip(t, axis=1)


def to_col_major(t):
    b, l = t.shape[0], t.shape[1]
    rows = l // GRID_W
    rest = t.shape[2:]
    return t.reshape(b, rows, GRID_W, *rest).swapaxes(1, 2).reshape(b, l, *rest)


def from_col_major(t):
    b, l = t.shape[0], t.shape[1]
    rows = l // GRID_W
    rest = t.shape[2:]
    return t.reshape(b, GRID_W, rows, *rest).swapaxes(1, 2).reshape(b, l, *rest)


def token_shift(z, w):
    zp = jnp.pad(z, ((0, 0), (1, 1), (0, 0)))
    return w[0] * zp[:, :-2] + w[1] * zp[:, 1:-1] + w[2] * zp[:, 2:]


def gla_chunk_scan(q, k, v, log_a, s0):
    b, l, h, dk = q.shape
    dv = v.shape[-1]
    nc = l // GLA_CHUNK
    q, k, log_a = [t.reshape(b, nc, GLA_CHUNK, h, dk) for t in (q, k, log_a)]
    v = v.reshape(b, nc, GLA_CHUNK, h, dv)
    cum = jnp.cumsum(log_a, axis=2)
    q_in = q * jnp.exp(cum)
    k_in = k * jnp.exp(-cum)
    lower = jnp.tril(jnp.ones((GLA_CHUNK, GLA_CHUNK), dtype=bool))
    att = jnp.where(lower, jnp.einsum('bnihk,bnjhk->bnhij', q_in, k_in), 0.0)
    o = jnp.einsum('bnhij,bnjhv->bnihv', att, v)
    last = cum[:, :, -1]
    kv = jnp.einsum('bnchk,bnchv->bnhkv', k * jnp.exp(last[:, :, None] - cum), v)

    def step(s, inp):
        dec, kv_n = inp
        return dec[..., None] * s + kv_n, s

    s_fin, s_prev = lax.scan(step, s0, (jnp.moveaxis(jnp.exp(last), 1, 0), jnp.moveaxis(kv, 1, 0)))
    o = o + jnp.einsum('bnchk,nbhkv->bnchv', q_in, s_prev)
    return o.reshape(b, l, h, dv), s_fin


def rwkv_scan(r, w, k, v, kk, a, s0):
    def step(s, inp):
        r_t, w_t, k_t, v_t, kk_t, a_t = inp
        sa = jnp.einsum('bhvk,bhk->bhv', s, -kk_t)
        s = s * w_t[:, :, None, :] + sa[..., None] * (kk_t * a_t)[:, :, None, :] + v_t[..., None] * k_t[:, :, None, :]
        return s, jnp.einsum('bhvk,bhk->bhv', s, r_t)

    xs = tuple(jnp.moveaxis(t, 1, 0) for t in (r, w, k, v, kk, a))
    s_fin, y = lax.scan(step, s0, xs)
    return jnp.moveaxis(y, 0, 1), s_fin


def s5_scan(u, lam_re, lam_im, log_step, b_re, b_im, s0):
    dt = jnp.exp(log_step)[:, None]
    mag = jnp.exp(lam_re * dt)
    a_re = mag * jnp.cos(lam_im * dt)
    a_im = mag * jnp.sin(lam_im * dt)
    den = lam_re * lam_re + lam_im * lam_im
    f_re = ((a_re - 1.0) * lam_re + a_im * lam_im) / den
    f_im = (a_im * lam_re - (a_re - 1.0) * lam_im) / den
    bb_re = f_re[..., None] * b_re - f_im[..., None] * b_im
    bb_im = f_re[..., None] * b_im + f_im[..., None] * b_re
    bu_re = jnp.einsum('gpc,blgc->blgp', bb_re, u)
    bu_im = jnp.einsum('gpc,blgc->blgp', bb_im, u)
    h_re, h_im = s0[:, 0], s0[:, 1]
    bu_re = bu_re.at[:, 0].add(a_re * h_re - a_im * h_im)
    bu_im = bu_im.at[:, 0].add(a_re * h_im + a_im * h_re)
    ar = jnp.broadcast_to(a_re, bu_re.shape)
    ai = jnp.broadcast_to(a_im, bu_re.shape)

    def combine(e1, e2):
        a1r, a1i, b1r, b1i = e1
        a2r, a2i, b2r, b2i = e2
        return (a2r * a1r - a2i * a1i, a2r * a1i + a2i * a1r,
                a2r * b1r - a2i * b1i + b2r, a2r * b1i + a2i * b1r + b2i)

    _, _, xr, xi = lax.associative_scan(combine, (ar, ai, bu_re, bu_im), axis=1)
    x = jnp.stack([xr, xi], axis=2)
    return x, x[:, -1]


def two_dir(scan_fn, ctx_f, lat_f, ctx_b, lat_b, p_f, p_b, s0):
    yc_f, sc_f = scan_fn(*ctx_f, *p_f, s0)
    yl_f, _ = scan_fn(*lat_f, *p_f, sc_f)
    yc_b, sc_b = scan_fn(*[flip_seq(t) for t in ctx_b], *p_b, s0)
    yl_b, _ = scan_fn(*[flip_seq(t) for t in lat_b], *p_b, sc_b)
    return yc_f + flip_seq(yc_b), yl_f + flip_seq(yl_b)


def token_mixers(hc, hl, need_ctx, w_in, gla_w_a2, gla_b_a, gla_norm_g, rwkv_conv, rwkv_w0, rwkv_w2,
                 rwkv_a0, rwkv_a2, rwkv_g2, rwkv_k_k, rwkv_k_a, rwkv_r_k, rwkv_ln_w, rwkv_ln_b,
                 s5_lam_re, s5_lam_im, s5_log_step, s5_b_re, s5_b_im, s5_c_re, s5_c_im, s5_d, s5_w_glu,
                 w_branch, w_out):
    bsz = hl.shape[0]
    zc = split_last(hc @ w_in, IN_SIZES)
    zl = split_last(hl @ w_in, IN_SIZES)

    def gla_prep(z, col_major):
        q, k, v, a_f, a_b = z[0], z[1], z[2], z[4], z[5]
        la = [jax.nn.log_sigmoid(heads(lr @ gla_w_a2[d] + gla_b_a[d], GLA_HEADS)) / GLA_TAU
              for d, lr in enumerate((a_f, a_b))]
        seq = [heads(q, GLA_HEADS) * GLA_DK ** -0.5, heads(k, GLA_HEADS), heads(v, GLA_HEADS)] + la
        return [to_col_major(t) for t in seq] if col_major else seq

    gc, gl = gla_prep(zc, False), gla_prep(zl, True)
    s0 = jnp.zeros((bsz, GLA_HEADS, GLA_DK, GLA_DV), jnp.float32)
    gla_c, gla_l = two_dir(gla_chunk_scan, gc[:4], gl[:4], gc[:3] + gc[4:], gl[:3] + gl[4:], (), (), s0)
    gla_l = from_col_major(gla_l)

    k_a = rwkv_k_a.reshape(RWKV_HEADS, RWKV_N)

    def rwkv_prep(zr):
        zr = token_shift(zr, rwkv_conv)
        r, k, v, w_f, w_b, a_f, a_b, g = split_last(zr, RWKV_SIZES)
        decay = [jnp.exp(-jnp.exp(-jax.nn.softplus(-heads(rwkv_w0[d] + jnp.tanh(lr) @ rwkv_w2[d], RWKV_HEADS)) - 0.5))
                 for d, lr in enumerate((w_f, w_b))]
        a = [jax.nn.sigmoid(heads(rwkv_a0[d] + lr @ rwkv_a2[d], RWKV_HEADS)) for d, lr in enumerate((a_f, a_b))]
        kk = heads(k * rwkv_k_k, RWKV_HEADS)
        kk = kk / jnp.maximum(jnp.sqrt(jnp.sum(kk * kk, axis=-1, keepdims=True)), 1e-12)
        kh, rh, vh = heads(k, RWKV_HEADS), heads(r, RWKV_HEADS), heads(v, RWKV_HEADS)
        kd = [kh * (1.0 + (ad - 1.0) * k_a) for ad in a]
        seqs = [(rh, decay[d], kd[d], vh, kk, a[d]) for d in range(2)]
        return seqs, (rh, kd[0] + kd[1], vh, g)

    (rc, ec), (rl, el) = rwkv_prep(zc[6]), rwkv_prep(zl[6])
    s0 = jnp.zeros((bsz, RWKV_HEADS, RWKV_N, RWKV_N), jnp.float32)
    rwkv_c, rwkv_l = two_dir(rwkv_scan, rc[0], rl[0], rc[1], rl[1], (), (), s0)

    uc, ul = heads(zc[7], S5_GROUPS), heads(zl[7], S5_GROUPS)
    p_f = (s5_lam_re[0], s5_lam_im[0], s5_log_step[0], s5_b_re, s5_b_im)
    p_b = (s5_lam_re[1], s5_lam_im[1], s5_log_step[1], s5_b_re, s5_b_im)
    s0 = jnp.zeros((bsz, 2, S5_GROUPS, S5_STATE), jnp.float32)
    x5_c, x5_l = two_dir(s5_scan, (uc,), (ul,), (uc,), (ul,), p_f, p_b, s0)

    def side_out(z, o_gla, y_rwkv, extra, x_s5, u):
        b, l = o_gla.shape[0], o_gla.shape[1]
        o_gla = o_gla * lax.rsqrt(jnp.mean(o_gla * o_gla, -1, keepdims=True) + 1e-6) * gla_norm_g
        br_gla = o_gla.reshape(b, l, GLA_V) * jax.nn.silu(z[3].astype(jnp.float32))
        rh, kd_sum, vh, g = extra
        mu = jnp.mean(y_rwkv, -1, keepdims=True)
        var = jnp.mean(jnp.square(y_rwkv - mu), -1, keepdims=True)
        y = ((y_rwkv - mu) * lax.rsqrt(var + RWKV_GN_EPS)).reshape(b, l, RWKV_W) * rwkv_ln_w + rwkv_ln_b
        y = y + (jnp.sum(rh * kd_sum * rwkv_r_k, -1, keepdims=True) * vh).reshape(b, l, RWKV_W)
        br_rwkv = y * (jax.nn.sigmoid(g) @ rwkv_g2)
        ys = (jnp.einsum('gcp,blgp->blgc', s5_c_re, x_s5[:, :, 0])
              - jnp.einsum('gcp,blgp->blgc', s5_c_im, x_s5[:, :, 1]))
        ys = (ys + s5_d.reshape(S5_GROUPS, S5_GROUP) * u).reshape(b, l, S5_WIDTH)
        lin, glu_gate = jnp.split(jax.nn.gelu(ys) @ s5_w_glu, 2, axis=-1)
        br_s5 = lin * jax.nn.sigmoid(glu_gate)
        gates = jax.nn.sigmoid(z[8].reshape(b, l, N_BRANCH, D_MODEL).astype(jnp.float32))
        br = jnp.einsum('blnc,ncd->blnd', jnp.stack([br_gla, br_rwkv, br_s5], axis=2), w_branch)
        return jnp.sum(gates * br, axis=2) @ w_out

    ml = side_out(zl, gla_l, rwkv_l, el, x5_l, ul)
    mc = side_out(zc, gla_c, rwkv_c, ec, x5_c, uc) if need_ctx else None
    return mc, ml


def peer_ffn(t, w_q, sub_keys, u, v):
    n = t.shape[0]
    q = (t @ w_q).reshape(n, PEER_HEADS, 2, PEER_HALF).astype(jnp.float32)
    s = jnp.einsum('nhsd,hskd->nhsk', q, sub_keys.astype(jnp.float32))
    s1, i1 = lax.top_k(s[:, :, 0], PEER_TOPK)
    s2, i2 = lax.top_k(s[:, :, 1], PEER_TOPK)
    cand = (s1[..., :, None] + s2[..., None, :]).reshape(n, PEER_HEADS, PEER_TOPK * PEER_TOPK)
    cidx = (i1[..., :, None] * PEER_NKEYS + i2[..., None, :]).reshape(n, PEER_HEADS, PEER_TOPK * PEER_TOPK)
    top, pos = lax.top_k(cand, PEER_TOPK)
    idx = jnp.take_along_axis(cidx, pos, axis=-1)
    g = jax.nn.softmax(top, axis=-1)
    nb = n // PEER_BLOCK

    def block(args):
        tb, ib, gb = args
        act = jax.nn.gelu(jnp.einsum('td,thkd->thk', tb, u[ib]).astype(jnp.float32))
        return jnp.einsum('thk,thkd->td', (gb * act).astype(v.dtype), v[ib])

    out = lax.map(block, (t.reshape(nb, PEER_BLOCK, t.shape[-1]),
                          idx.reshape(nb, PEER_BLOCK, PEER_HEADS, PEER_TOPK),
                          g.reshape(nb, PEER_BLOCK, PEER_HEADS, PEER_TOPK)))
    return out.reshape(n, -1)


def setup_inputs(seed: int = 0) -> dict:
    key = jax.random.key(seed)
    keys = iter(jax.random.split(key, 48))
    f32 = jnp.float32

    def nrm(shape, scale):
        return jax.random.normal(next(keys), shape, f32) * scale

    L, D = DEPTH, D_MODEL
    log_lo, log_hi = math.log(1e-3), math.log(1e-1)
    return {
        'x': nrm((BATCH, SEQ, D), 1.0),
        'c': nrm((BATCH, D), 1.0),
        'ctx': nrm((BATCH, CTX_LEN, D), 1.0),
        'c_ctx': nrm((D,), 1.0),
        'w_mod': nrm((L, D, N_MOD * D), 0.5 * D ** -0.5),
        'b_mod': nrm((L, N_MOD * D), 0.02),
        'norm1_g': 1.0 + nrm((L, D), 0.02),
        'norm2_g': 1.0 + nrm((L, D), 0.02),
        'w_in': nrm((L, D, D_IN), D ** -0.5),
        'gla_w_a2': nrm((L, 2, GLA_GATE_RANK, GLA_K), GLA_GATE_RANK ** -0.5),
        'gla_b_a': nrm((L, 2, GLA_K), 0.1),
        'gla_norm_g': 1.0 + nrm((L, GLA_DV), 0.02),
        'rwkv_conv': jnp.array([0.25, 0.5, 0.25], f32)[None, :, None] + nrm((L, 3, RWKV_IN), 0.05),
        'rwkv_w0': jnp.linspace(-6.5, -1.5, RWKV_W, dtype=f32) + nrm((L, 2, RWKV_W), 0.1),
        'rwkv_w2': nrm((L, 2, RWKV_DECAY_RANK, RWKV_W), 0.1),
        'rwkv_a0': nrm((L, 2, RWKV_W), 0.1),
        'rwkv_a2': nrm((L, 2, RWKV_A_RANK, RWKV_W), 0.1),
        'rwkv_g2': nrm((L, RWKV_GATE_RANK, RWKV_W), RWKV_GATE_RANK ** -0.5),
        'rwkv_k_k': 0.85 + nrm((L, RWKV_W), 0.02),
        'rwkv_k_a': 1.0 + nrm((L, RWKV_W), 0.02),
        'rwkv_r_k': nrm((L, RWKV_HEADS, RWKV_N), 0.1),
        'rwkv_ln_w': 1.0 + nrm((L, RWKV_W), 0.02),
        'rwkv_ln_b': nrm((L, RWKV_W), 0.02),
        's5_lam_re': -0.5 + nrm((L, 2, S5_GROUPS, S5_STATE), 0.01),
        's5_lam_im': math.pi * jnp.arange(S5_STATE, dtype=f32) + nrm((L, 2, S5_GROUPS, S5_STATE), 0.01),
        's5_log_step': log_lo + (log_hi - log_lo) * jax.random.uniform(next(keys), (L, 2, S5_GROUPS), f32),
        's5_b_re': nrm((L, S5_GROUPS, S5_STATE, S5_GROUP), (2 * S5_GROUP) ** -0.5),
        's5_b_im': nrm((L, S5_GROUPS, S5_STATE, S5_GROUP), (2 * S5_GROUP) ** -0.5),
        's5_c_re': nrm((L, S5_GROUPS, S5_GROUP, S5_STATE), S5_STATE ** -0.5),
        's5_c_im': nrm((L, S5_GROUPS, S5_GROUP, S5_STATE), S5_STATE ** -0.5),
        's5_d': nrm((L, S5_WIDTH), 0.5),
        's5_w_glu': nrm((L, S5_WIDTH, 2 * S5_WIDTH), S5_WIDTH ** -0.5),
        'w_branch': nrm((L, N_BRANCH, BRANCH_W, D), BRANCH_W ** -0.5),
        'w_out': nrm((L, D, D), D ** -0.5),
        'peer_w_q': nrm((L, D, PEER_HEADS * PEER_DQ), D ** -0.5),
        'peer_sub_keys': nrm((L, PEER_HEADS, 2, PEER_NKEYS, PEER_HALF), PEER_HALF ** -0.5),
        'peer_u': nrm((L, PEER_EXPERTS, D), D ** -0.5),
        'peer_v': nrm((L, PEER_EXPERTS, D), PEER_HEADS ** -0.5),
        'final_g': 1.0 + nrm((D,), 0.02),
    }


def reference(x, c, ctx, c_ctx, w_mod, b_mod, norm1_g, norm2_g, w_in, gla_w_a2, gla_b_a, gla_norm_g,
              rwkv_conv, rwkv_w0, rwkv_w2, rwkv_a0, rwkv_a2, rwkv_g2, rwkv_k_k, rwkv_k_a, rwkv_r_k,
              rwkv_ln_w, rwkv_ln_b, s5_lam_re, s5_lam_im, s5_log_step, s5_b_re, s5_b_im, s5_c_re, s5_c_im,
              s5_d, s5_w_glu, w_branch, w_out, peer_w_q, peer_sub_keys, peer_u, peer_v, final_g):
    xl, xc = x, ctx
    for i in range(DEPTH):
        last = i == DEPTH - 1
        mod_l = (jax.nn.silu(c) @ w_mod[i] + b_mod[i])[:, None, :]
        mod_c = jax.nn.silu(c_ctx) @ w_mod[i] + b_mod[i]
        sh1, sc1, g1, sh2, sc2, g2 = jnp.split(mod_l, N_MOD, axis=-1)
        csh1, csc1, cg1, csh2, csc2, cg2 = jnp.split(mod_c, N_MOD, axis=-1)
        mc, ml = token_mixers(
            modulate(xc, norm1_g[i], csh1, csc1), modulate(xl, norm1_g[i], sh1, sc1), not last,
            w_in[i], gla_w_a2[i], gla_b_a[i], gla_norm_g[i], rwkv_conv[i], rwkv_w0[i], rwkv_w2[i],
            rwkv_a0[i], rwkv_a2[i], rwkv_g2[i], rwkv_k_k[i], rwkv_k_a[i], rwkv_r_k[i], rwkv_ln_w[i],
            rwkv_ln_b[i], s5_lam_re[i], s5_lam_im[i], s5_log_step[i], s5_b_re[i], s5_b_im[i],
            s5_c_re[i], s5_c_im[i], s5_d[i], s5_w_glu[i], w_branch[i], w_out[i])
        xl = xl + g1 * ml
        hl = modulate(xl, norm2_g[i], sh2, sc2)
        if last:
            f = peer_ffn(hl.reshape(-1, D_MODEL), peer_w_q[i], peer_sub_keys[i], peer_u[i], peer_v[i])
            xl = xl + g2 * f.reshape(xl.shape)
        else:
            xc = xc + cg1 * mc
            hc = modulate(xc, norm2_g[i], csh2, csc2)
            n_c = hc.shape[0] * hc.shape[1]
            f = peer_ffn(jnp.concatenate([hc.reshape(-1, D_MODEL), hl.reshape(-1, D_MODEL)], axis=0),
                         peer_w_q[i], peer_sub_keys[i], peer_u[i], peer_v[i])
            xc = xc + cg2 * f[:n_c].reshape(xc.shape)
            xl = xl + g2 * f[n_c:].reshape(xl.shape)
    return rms_norm(xl, final_g)
```

```python
import functools
import math

import jax
import jax.numpy as jnp
import numpy as np
from jax import lax
from jax.experimental import pallas as pl
from jax.experimental.pallas import tpu as pltpu

F32 = jnp.float32
BF16 = jnp.bfloat16

D_MODEL = 1024
GRID_W = 64
CHUNK = 64
TOKEN_TILE = 256
SCAN_CHUNKS = 4

GLA_HEADS, GLA_DK, GLA_DV, GLA_RANK, GLA_TAU = 4, 64, 128, 16, 16.0
RWKV_HEADS, RWKV_N, RWKV_W = 8, 64, 512
RWKV_GN_EPS = 64e-5
S5_WIDTH, S5_GROUP, S5_GROUPS, S5_STATE = 512, 16, 32, 64
PEER_HEADS, PEER_NKEYS, PEER_HALF, PEER_TOPK = 8, 128, 128, 16
PEER_EXPERT_BLOCK = 1024

Z_QKV, Z_GG, Z_S5, Z_RWKV, Z_GATE, Z_ALR, Z_WIDTH = 0, 1024, 1536, 2048, 4096, 7168, 7296
Z_NBLOCK = 2432
VMEM_LIMIT = 52 * 1024 * 1024

NN = (((1,), (0,)), ((), ()))
NT = (((1,), (1,)), ((), ()))
TN = (((0,), (0,)), ((), ()))


def _dg(a, b, dims):
    return lax.dot_general(a, b, dims, preferred_element_type=F32)


def _dot(a, b, dims=NN):
    return _dg(a.astype(BF16), b.astype(BF16), dims)


def _split2(a):
    hi = a.astype(BF16)
    lo = (a - hi.astype(F32)).astype(BF16)
    return hi, lo


def _split3(a):
    p1 = a.astype(BF16)
    r1 = a - p1.astype(F32)
    p2 = r1.astype(BF16)
    p3 = (r1 - p2.astype(F32)).astype(BF16)
    return p1, p2, p3


def _dot3(a, b, dims=NN):
    ah, al = _split2(a)
    bh, bl = _split2(b)
    return _dg(ah, bh, dims) + (_dg(ah, bl, dims) + _dg(al, bh, dims))


def _dot_exact_lhs(a_bf, b, dims=NN):
    b1, b2, b3 = _split3(b)
    return _dg(a_bf, b1, dims) + (_dg(a_bf, b2, dims) + _dg(a_bf, b3, dims))


def _dot_exact_rhs(a, b_bf, dims=NN):
    a1, a2, a3 = _split3(a)
    return _dg(a1, b_bf, dims) + (_dg(a2, b_bf, dims) + _dg(a3, b_bf, dims))


def _sigmoid(x):
    return 1.0 / (1.0 + jnp.exp(-x))


def _softplus(x):
    return jnp.maximum(x, 0.0) + jnp.log(1.0 + jnp.exp(-jnp.abs(x)))


def _gelu(x):
    c = math.sqrt(2.0 / math.pi)
    return x * (0.5 * (1.0 + jnp.tanh(c * (x + 0.044715 * (x * x * x)))))


def _rms(x, eps=1e-6):
    return x * lax.rsqrt(jnp.mean(x * x, axis=-1, keepdims=True) + eps)


def _params(*sem):
    return pltpu.CompilerParams(dimension_semantics=sem, vmem_limit_bytes=VMEM_LIMIT)


def _full(shape):
    n = len(shape)
    return pl.BlockSpec(shape, lambda *_: (0,) * n)


def _mod_body(c_ref, w_ref, b_ref, o_ref):
    c = c_ref[...]
    s = c * _sigmoid(c)
    o_ref[...] = _dot3(s, w_ref[...]) + b_ref[...]


def _mod_vectors(cc, w_mod, b_mod):
    depth, d, n = w_mod.shape
    nb = 1024
    return pl.pallas_call(
        _mod_body,
        grid=(depth, n // nb),
        in_specs=[
            pl.BlockSpec((8, d), lambda l, j: (0, 0)),
            pl.BlockSpec((None, d, nb), lambda l, j: (l, 0, j)),
            pl.BlockSpec((None, 1, nb), lambda l, j: (l, 0, j)),
        ],
        out_specs=pl.BlockSpec((None, 8, nb), lambda l, j: (l, 0, j)),
        out_shape=jax.ShapeDtypeStruct((depth, 8, n), F32),
        compiler_params=_params("parallel", "parallel"),
        name="mod_vectors",
    )(cc, w_mod, b_mod.reshape(depth, 1, n))


def _mod_row(mod_ref, tile_idx, n_ctx_tiles, k):
    row = jnp.where(tile_idx < n_ctx_tiles, 1, 0)
    return mod_ref[pl.ds(row, 1), k * D_MODEL:(k + 1) * D_MODEL]


def _inproj_body(x_ref, mod_ref, g_ref, w_ref, o_ref, *, n_ctx_tiles):
    i = pl.program_id(1)
    y = _rms(x_ref[...]) * g_ref[...]
    h = y * (1.0 + _mod_row(mod_ref, i, n_ctx_tiles, 1)) + _mod_row(mod_ref, i, n_ctx_tiles, 0)
    o_ref[...] = _dot(h, w_ref[...])


def _in_projection(x, mod, g, w_bf, n_ctx_tiles):
    ntok, d = x.shape
    n = w_bf.shape[1]
    tm, nb = TOKEN_TILE, Z_NBLOCK
    return pl.pallas_call(
        functools.partial(_inproj_body, n_ctx_tiles=n_ctx_tiles),
        grid=(n // nb, ntok // tm),
        in_specs=[
            pl.BlockSpec((tm, d), lambda j, i: (i, 0)),
            _full(mod.shape),
            _full(g.shape),
            pl.BlockSpec((d, nb), lambda j, i: (0, j)),
        ],
        out_specs=pl.BlockSpec((tm, nb), lambda j, i: (i, j)),
        out_shape=jax.ShapeDtypeStruct((ntok, n), F32),
        compiler_params=_params("parallel", "parallel"),
        name="in_projection",
    )(x, mod, g, w_bf)


def _prep_body(zr_ref, zp_ref, zn_ref, alr_ref, conv_ref, w0_ref, w2_ref, a0_ref, a2_ref, kk_ref, ka_ref,
               bd_ref, gw_ref, gb_ref, pc_ref, pf_ref, pb_ref, ex_ref, la_ref):
    conv = conv_ref[...]
    zs = conv[0:1] * zp_ref[...] + conv[1:2] * zr_ref[...] + conv[2:3] * zn_ref[...]
    r = zs[:, 0:512]
    k = zs[:, 512:1024]
    v = zs[:, 1024:1536]
    wlr = jnp.tanh(zs[:, 1536:1664])
    alr = zs[:, 1664:1792]
    kk = k * kk_ref[...]
    nrm = jnp.sqrt(_dot_exact_rhs(kk * kk, bd_ref[...]))
    kk = kk / jnp.maximum(nrm, 1e-12)
    pc_ref[:, 0:512] = r
    pc_ref[:, 512:1024] = v
    pc_ref[:, 1024:1536] = kk
    kd_sum = jnp.zeros_like(k)
    for d, p_ref in enumerate((pf_ref, pb_ref)):
        xw = w0_ref[d:d + 1, :] + _dot3(wlr, w2_ref[d])
        p_ref[:, 0:512] = -jnp.exp(-_softplus(-xw) - 0.5)
        a = _sigmoid(a0_ref[d:d + 1, :] + _dot3(alr, a2_ref[d]))
        kd = k * (1.0 + (a - 1.0) * ka_ref[...])
        kd_sum = kd_sum + kd
        p_ref[:, 512:1024] = kd
        p_ref[:, 1024:1536] = kk * a
    ex_ref[:, 0:512] = kd_sum
    ex_ref[:, 512:640] = zs[:, 1792:1920]
    ga = alr_ref[...]
    for d in range(2):
        pre = _dot3(ga, gw_ref[d]) + gb_ref[d:d + 1, :]
        la_ref[:, d * 256:(d + 1) * 256] = -_softplus(-pre) * (1.0 / GLA_TAU)


def _mixer_prep(z, z_prev, z_next, conv, w0, w2p, a0, a2p, k_k, k_a, bd_ones, gwp, gb):
    ntok = z.shape[0]
    tm = TOKEN_TILE
    rw = 2048
    tok = lambda w, j: pl.BlockSpec((tm, w), lambda i: (i, j))
    return pl.pallas_call(
        _prep_body,
        grid=(ntok // tm,),
        in_specs=[
            tok(rw, Z_RWKV // rw), tok(rw, 0), tok(rw, 0), tok(128, Z_ALR // 128),
            _full(conv.shape), _full(w0.shape), _full(w2p.shape), _full(a0.shape), _full(a2p.shape),
            _full(k_k.shape), _full(k_a.shape), _full(bd_ones.shape), _full(gwp.shape), _full(gb.shape),
        ],
        out_specs=[tok(1536, 0), tok(1536, 0), tok(1536, 0), tok(640, 0), tok(512, 0)],
        out_shape=[
            jax.ShapeDtypeStruct((ntok, 1536), F32),
            jax.ShapeDtypeStruct((ntok, 1536), F32),
            jax.ShapeDtypeStruct((ntok, 1536), F32),
            jax.ShapeDtypeStruct((ntok, 640), F32),
            jax.ShapeDtypeStruct((ntok, 512), F32),
        ],
        compiler_params=_params("parallel"),
        name="mixer_prep",
    )(z, z_prev, z_next, z, conv, w0, w2p, a0, a2p, k_k, k_a, bd_ones, gwp, gb)


def _chunk_masks():
    row = lax.broadcasted_iota(jnp.int32, (CHUNK, CHUNK), 0)
    col = lax.broadcasted_iota(jnp.int32, (CHUNK, CHUNK), 1)
    return row > col, row >= col, row == col


INV_BASE_LOG2 = 4


def _unit_lower_inverse(a, eye_f):
    row = lax.broadcasted_iota(jnp.int32, (CHUNK, CHUNK), 0)
    col = lax.broadcasted_iota(jnp.int32, (CHUNK, CHUNK), 1)
    same = lambda s: (row >> s) == (col >> s)
    ap = jnp.where(same(INV_BASE_LOG2), a, 0.0)
    inv = eye_f + ap
    for _ in range(INV_BASE_LOG2 - 1):
        ap = _dot(ap, ap)
        inv = inv + _dot(inv, ap)
    for s in range(INV_BASE_LOG2, int(math.log2(CHUNK))):
        off = jnp.where(same(s + 1) & jnp.logical_not(same(s)), a, 0.0)
        inv = inv + _dot(inv, _dot(off, inv))
    return inv


def _gla_body(q_ref, k_ref, v_ref, la_ref, o_ref, st_ref):
    @pl.when(pl.program_id(1) == 0)
    def _():
        st_ref[...] = jnp.zeros_like(st_ref)

    _, incl, _ = _chunk_masks()
    tri = jnp.where(incl, 1.0, 0.0).astype(BF16)
    st = st_ref[...]
    for c in range(SCAN_CHUNKS):
        sl = slice(c * CHUNK, (c + 1) * CHUNK)
        q, k, v, la = q_ref[sl, :], k_ref[sl, :], v_ref[sl, :], la_ref[sl, :]
        cum = _dot_exact_lhs(tri, la)
        last = cum[CHUNK - 1:CHUNK, :]
        q_in = q * (GLA_DK ** -0.5) * jnp.exp(cum)
        k_in = k * jnp.exp(-cum)
        k_end = k * jnp.exp(last - cum)
        att = jnp.where(incl, _dot(q_in, k_in, NT), 0.0)
        o_ref[sl, :] = _dot(att, v) + _dot(q_in, st, NT)
        st = st * jnp.exp(last) + _dot(v, k_end, TN)
    st_ref[...] = st


def _gla_scan(q, k, v, la):
    nchain, npos, _ = q.shape
    rows = SCAN_CHUNKS * CHUNK
    spec = lambda w: pl.BlockSpec((None, rows, w), lambda b, i: (b, i, 0))
    return pl.pallas_call(
        _gla_body,
        grid=(nchain, npos // rows),
        in_specs=[spec(GLA_DK), spec(GLA_DK), spec(GLA_DV), spec(GLA_DK)],
        out_specs=spec(GLA_DV),
        out_shape=jax.ShapeDtypeStruct((nchain, npos, GLA_DV), F32),
        scratch_shapes=[pltpu.VMEM((GLA_DV, GLA_DK), F32)],
        compiler_params=_params("parallel", "arbitrary"),
        name="gla_scan",
    )(q, k, v, la)


def _rwkv_body(r_ref, v_ref, kk_ref, lw_ref, kd_ref, bt_ref, y_ref, t_ref):
    @pl.when(pl.program_id(1) == 0)
    def _():
        t_ref[...] = jnp.zeros_like(t_ref)

    strict, incl, eye = _chunk_masks()
    tri = jnp.where(incl, 1.0, 0.0).astype(BF16)
    eye_f = jnp.where(eye, 1.0, 0.0)
    t = t_ref[...]
    for c in range(SCAN_CHUNKS):
        sl = slice(c * CHUNK, (c + 1) * CHUNK)
        r, v, kk, lw, kd, bt = (ref[sl, :] for ref in (r_ref, v_ref, kk_ref, lw_ref, kd_ref, bt_ref))
        cl = _dot_exact_lhs(tri, lw)
        e_pos = jnp.exp(cl)
        e_neg = jnp.exp(-cl)
        al = -kk * jnp.exp(cl - lw)
        bq = bt * e_neg
        kq = kd * e_neg
        rq = r * e_pos
        g_end = e_pos[CHUNK - 1:CHUNK, :]
        a_ab = jnp.where(strict, _dot(al, bq, NT), 0.0)
        a_ak = jnp.where(strict, _dot(al, kq, NT), 0.0)
        p_b = jnp.where(incl, _dot(rq, bq, NT), 0.0)
        p_k = jnp.where(incl, _dot(rq, kq, NT), 0.0)
        inv = _unit_lower_inverse(a_ab, eye_f)
        a_hat = _dot(inv, al)
        u0 = _dot(inv, _dot(a_ak, v))
        y0 = _dot(p_k, v) + _dot(p_b, u0)
        r_hat = _dot(p_b, a_hat) + rq
        b_end = bq * g_end
        k_end = kq * g_end
        t_add = _dot(k_end, v, TN) + _dot(b_end, u0, TN)
        g = _dot(b_end, a_hat, TN) + eye_f * g_end
        y_ref[sl, :] = y0 + _dot(r_hat, t)
        t = _dot3(g, t) + t_add
    t_ref[...] = t


def _rwkv_scan(r, v, kk, lw, kd, bt):
    nchain, npos, n = r.shape
    rows = SCAN_CHUNKS * CHUNK
    spec = pl.BlockSpec((None, rows, n), lambda b, i: (b, i, 0))
    return pl.pallas_call(
        _rwkv_body,
        grid=(nchain, npos // rows),
        in_specs=[spec] * 6,
        out_specs=spec,
        out_shape=jax.ShapeDtypeStruct((nchain, npos, n), F32),
        scratch_shapes=[pltpu.VMEM((n, n), F32)],
        compiler_params=_params("parallel", "arbitrary"),
        name="rwkv_scan",
    )(r, v, kk, lw, kd, bt)


def _s5_body(u_ref, tz_ref, fm_ref, em_ref, a1_ref, a2_ref, y_ref, f_scr, x_scr, *, n_chunks):
    u = u_ref[...].astype(BF16)
    f_scr[...] = _dg(u, fm_ref[...], NN)
    a1 = a1_ref[...]
    a2 = a2_ref[...]

    def step(c, x):
        x_scr[pl.ds(c, 1), :] = x
        return x * a1 + pltpu.roll(x, S5_STATE, 1) * a2 + f_scr[pl.ds(c, 1), :]

    lax.fori_loop(0, n_chunks, step, jnp.zeros((1, 2 * S5_STATE), F32))
    y_ref[...] = _dg(u, tz_ref[...], NN) + _dot(x_scr[...], em_ref[...])


def _s5_scan(u, tz, fm, em, a1, a2):
    nchain, n_chunks, w = u.shape
    p2 = 2 * S5_STATE
    return pl.pallas_call(
        functools.partial(_s5_body, n_chunks=n_chunks),
        grid=(nchain,),
        in_specs=[
            pl.BlockSpec((None, n_chunks, w), lambda b: (b, 0, 0)),
            pl.BlockSpec((None, w, w), lambda b: (b, 0, 0)),
            pl.BlockSpec((None, w, p2), lambda b: (b, 0, 0)),
            pl.BlockSpec((None, p2, w), lambda b: (b, 0, 0)),
            pl.BlockSpec((None, 1, p2), lambda b: (b, 0, 0)),
            pl.BlockSpec((None, 1, p2), lambda b: (b, 0, 0)),
        ],
        out_specs=pl.BlockSpec((None, n_chunks, w), lambda b: (b, 0, 0)),
        out_shape=jax.ShapeDtypeStruct((nchain, n_chunks, w), F32),
        scratch_shapes=[pltpu.VMEM((n_chunks, p2), F32), pltpu.VMEM((n_chunks, p2), F32)],
        compiler_params=_params("parallel"),
        name="s5_scan",
    )(u, tz, fm, em, a1, a2)


def _side_body(x_ref, zgg_ref, zu_ref, zg0_ref, zg1_ref, zg2_ref, of_ref, ob_ref, yf_ref, yb_ref, r_ref,
               v_ref, ex_ref, sf_ref, sb_ref, mod_ref, gng_ref, lnw_ref, lnb_ref, rk_ref, g2_ref, s5d_ref,
               wglu_ref, wbr_ref, wout_ref, n2g_ref, bda_ref, bdo_ref, xo_ref, hl_ref, *, n_ctx_tiles):
    i = pl.program_id(0)
    o = of_ref[...] + ob_ref[...]
    og = jnp.concatenate(
        [_rms(o[:, h * GLA_DV:(h + 1) * GLA_DV]) * gng_ref[...] for h in range(GLA_HEADS)], axis=1)
    zgg = zgg_ref[...]
    br_gla = og * (zgg * _sigmoid(zgg))
    y = yf_ref[...] + yb_ref[...]
    bda = bda_ref[...]
    mu = _dot_exact_rhs(y, bda)
    dy = y - mu
    var = _dot_exact_rhs(dy * dy, bda)
    yn = dy * lax.rsqrt(var + RWKV_GN_EPS) * lnw_ref[...] + lnb_ref[...]
    bonus = _dot_exact_rhs(r_ref[...] * ex_ref[:, 0:512] * rk_ref[...], bdo_ref[...])
    yn = yn + bonus * v_ref[...]
    br_rwkv = yn * _dot(_sigmoid(ex_ref[:, 512:640]), g2_ref[...])
    ys = sf_ref[...] + sb_ref[...] + s5d_ref[...] * zu_ref[...]
    lg = _dot(_gelu(ys), wglu_ref[...])
    br_s5 = lg[:, 0:S5_WIDTH] * _sigmoid(lg[:, S5_WIDTH:2 * S5_WIDTH])
    m = (_sigmoid(zg0_ref[...]) * _dot(br_gla, wbr_ref[0])
         + _sigmoid(zg1_ref[...]) * _dot(br_rwkv, wbr_ref[1])
         + _sigmoid(zg2_ref[...]) * _dot(br_s5, wbr_ref[2]))
    xn = x_ref[...] + _mod_row(mod_ref, i, n_ctx_tiles, 2) * _dot(m, wout_ref[...])
    xo_ref[...] = xn
    hl_ref[...] = (_rms(xn) * n2g_ref[...] * (1.0 + _mod_row(mod_ref, i, n_ctx_tiles, 4))
                   + _mod_row(mod_ref, i, n_ctx_tiles, 3))


def _side_out(x, z, o_f, o_b, y_f, y_b, pc, ex, s_f, s_b, mod, consts, n_ctx_tiles):
    ntok = x.shape[0]
    tm = TOKEN_TILE
    tok = lambda w, j: pl.BlockSpec((tm, w), lambda i: (i, j))
    return pl.pallas_call(
        functools.partial(_side_body, n_ctx_tiles=n_ctx_tiles),
        grid=(ntok // tm,),
        in_specs=[
            tok(D_MODEL, 0),
            tok(512, Z_GG // 512), tok(512, Z_S5 // 512),
            tok(1024, Z_GATE // 1024), tok(1024, Z_GATE // 1024 + 1), tok(1024, Z_GATE // 1024 + 2),
            tok(512, 0), tok(512, 0), tok(512, 0), tok(512, 0),
            tok(512, 0), tok(512, 1), tok(640, 0), tok(512, 0), tok(512, 0),
            _full(mod.shape),
        ] + [_full(c.shape) for c in consts],
        out_specs=[tok(D_MODEL, 0), tok(D_MODEL, 0)],
        out_shape=[jax.ShapeDtypeStruct((ntok, D_MODEL), F32)] * 2,
        compiler_params=_params("parallel"),
        name="side_out",
    )(x, z, z, z, z, z, o_f, o_b, y_f, y_b, pc, pc, ex, s_f, s_b, mod, *consts)


def _top_values(x, k):
    out = []
    for _ in range(k):
        m = jnp.max(x, axis=0, keepdims=True)
        out.append(m)
        x = jnp.where(x == m, -jnp.inf, x)
    return out


def _peer_score_body(h_ref, wh_ref, wl_ref, keys_ref, s1_ref, s2_ref, e1_ref, e2_ref, tau_ref):
    hh, hl = _split2(h_ref[...])
    for hd in range(PEER_HEADS):
        tops = []
        for s, s_ref in enumerate((s1_ref, s2_ref)):
            rows = slice((hd * 2 + s) * PEER_HALF, (hd * 2 + s + 1) * PEER_HALF)
            wh, wl = wh_ref[rows, :], wl_ref[rows, :]
            q_t = _dg(wh, hh, NT) + (_dg(wh, hl, NT) + _dg(wl, hh, NT))
            sc = _dot3(keys_ref[hd, s], q_t)
            s_ref[hd] = sc
            tops.append(_top_values(sc, PEER_TOPK))
        t1, t2 = tops
        t2_all = jnp.concatenate(t2, axis=0)
        cand = jnp.concatenate([t1[a] + t2_all for a in range(PEER_TOPK)], axis=0)
        best = _top_values(cand, PEER_TOPK)
        zsum = jnp.ones_like(best[0])
        for b in best[1:]:
            zsum = zsum + jnp.exp(b - best[0])
        tau_ref[hd] = best[-1]
        e1_ref[hd] = jnp.exp(s1_ref[hd] - t1[0])
        e2_ref[hd] = jnp.exp(s2_ref[hd] - t2[0]) / zsum


def _peer_scores(h, wq_hi, wq_lo, keys):
    ntok = h.shape[0]
    tm = TOKEN_TILE
    big = pl.BlockSpec((PEER_HEADS, PEER_NKEYS, tm), lambda i: (0, 0, i))
    big_shape = jax.ShapeDtypeStruct((PEER_HEADS, PEER_NKEYS, ntok), F32)
    return pl.pallas_call(
        _peer_score_body,
        grid=(ntok // tm,),
        in_specs=[pl.BlockSpec((tm, D_MODEL), lambda i: (i, 0)),
                  _full(wq_hi.shape), _full(wq_lo.shape), _full(keys.shape)],
        out_specs=[big, big, big, big, pl.BlockSpec((PEER_HEADS, 1, tm), lambda i: (0, 0, i))],
        out_shape=[big_shape] * 4 + [jax.ShapeDtypeStruct((PEER_HEADS, 1, ntok), F32)],
        compiler_params=_params("parallel"),
        name="peer_scores",
    )(h, wq_hi, wq_lo, keys)


def _peer_expert_body(h_ref, s1_ref, s2_ref, e1_ref, e2_ref, tau_ref, u_ref, vt_ref, o_ref, g_scr):
    j = pl.program_id(1)

    @pl.when(j == 0)
    def _():
        o_ref[...] = jnp.zeros_like(o_ref)

    n_i1 = PEER_EXPERT_BLOCK // PEER_NKEYS
    for il in range(n_i1):
        i1 = j * n_i1 + il
        acc = None
        for hd in range(PEER_HEADS):
            sel = (s2_ref[hd] + s1_ref[hd, pl.ds(i1, 1), :]) >= tau_ref[hd]
            w = jnp.where(sel, e2_ref[hd] * e1_ref[hd, pl.ds(i1, 1), :], 0.0)
            acc = w if acc is None else acc + w
        g_scr[il * PEER_NKEYS:(il + 1) * PEER_NKEYS, :] = acc
    act = _gelu(_dg(u_ref[...], h_ref[...], NT))
    o_ref[...] += _dg(vt_ref[...], (g_scr[...] * act).astype(BF16), NN)


def _peer_experts(h_bf, s1, s2, e1, e2, tau, u_bf, vt_bf):
    ntok = h_bf.shape[0]
    n_exp = u_bf.shape[0]
    tm, eb = TOKEN_TILE, PEER_EXPERT_BLOCK
    big = pl.BlockSpec((PEER_HEADS, PEER_NKEYS, tm), lambda i, j: (0, 0, i))
    return pl.pallas_call(
        _peer_expert_body,
        grid=(ntok // tm, n_exp // eb),
        in_specs=[
            pl.BlockSpec((tm, D_MODEL), lambda i, j: (i, 0)),
            big, big, big, big,
            pl.BlockSpec((PEER_HEADS, 1, tm), lambda i, j: (0, 0, i)),
            pl.BlockSpec((eb, D_MODEL), lambda i, j: (j, 0)),
            pl.BlockSpec((D_MODEL, eb), lambda i, j: (0, j)),
        ],
        out_specs=pl.BlockSpec((D_MODEL, tm), lambda i, j: (0, i)),
        out_shape=jax.ShapeDtypeStruct((D_MODEL, ntok), F32),
        scratch_shapes=[pltpu.VMEM((eb, tm), F32)],
        compiler_params=_params("parallel", "arbitrary"),
        name="peer_experts",
    )(h_bf, s1, s2, e1, e2, tau, u_bf, vt_bf)


def _residual_body(x_ref, f_ref, mod_ref, g_ref, o_ref, *, n_ctx_tiles, final):
    xn = x_ref[...] + _mod_row(mod_ref, pl.program_id(0), n_ctx_tiles, 5) * f_ref[...]
    if final:
        xn = _rms(xn) * g_ref[...]
    o_ref[...] = xn


def _residual(x, f, mod, g, n_ctx_tiles, final):
    ntok = x.shape[0]
    tm = TOKEN_TILE
    tok = pl.BlockSpec((tm, D_MODEL), lambda i: (i, 0))
    return pl.pallas_call(
        functools.partial(_residual_body, n_ctx_tiles=n_ctx_tiles, final=final),
        grid=(ntok // tm,),
        in_specs=[tok, tok, _full(mod.shape), _full(g.shape)],
        out_specs=tok,
        out_shape=jax.ShapeDtypeStruct((ntok, D_MODEL), F32),
        compiler_params=_params("parallel"),
        name="residual",
    )(x, f, mod, g)


def _pad_rows(w, rows, offset):
    return jnp.zeros((rows, w.shape[1]), w.dtype).at[offset:offset + w.shape[0]].set(w)


def _scan_order(a, n_ctx, col_major, reverse):
    c, l = a[:n_ctx], a[n_ctx:]
    if col_major:
        w = l.shape[1]
        l = l.reshape(-1, GRID_W, w).swapaxes(0, 1).reshape(-1, w)
    if reverse:
        c, l = c[::-1], l[::-1]
    return jnp.concatenate([c, l], axis=0)


def _raster_order(a, n_ctx, col_major, reverse):
    c, l = a[:n_ctx], a[n_ctx:]
    if reverse:
        c, l = c[::-1], l[::-1]
    if col_major:
        w = l.shape[1]
        l = l.reshape(GRID_W, -1, w).swapaxes(0, 1).reshape(-1, w)
    return jnp.concatenate([c, l], axis=0)


def _to_heads(a, n_heads):
    n, w = a.shape
    return a.reshape(n, n_heads, w // n_heads).transpose(1, 0, 2)


def _from_heads(a):
    h, n, w = a.shape
    return a.transpose(1, 0, 2).reshape(n, h * w)


def _s5_operators(lam_re, lam_im, log_step, b_re, b_im, c_re, c_im):
    c_len = CHUNK
    dt = jnp.exp(log_step)[..., None]
    mag = jnp.exp(lam_re * dt)
    a_re, a_im = mag * jnp.cos(lam_im * dt), mag * jnp.sin(lam_im * dt)
    den = lam_re * lam_re + lam_im * lam_im
    f_re = ((a_re - 1.0) * lam_re + a_im * lam_im) / den
    f_im = (a_im * lam_re - (a_re - 1.0) * lam_im) / den
    bb_re = f_re[..., None] * b_re - f_im[..., None] * b_im
    bb_im = f_re[..., None] * b_im + f_im[..., None] * b_re
    tau = jnp.arange(c_len + 1, dtype=F32)[:, None, None, None]
    pmag = jnp.exp(lam_re * dt * tau)
    p_re, p_im = pmag * jnp.cos(lam_im * dt * tau), pmag * jnp.sin(lam_im * dt * tau)
    cp_re = c_re[None, None] * p_re[:, :, :, None, :] - c_im[None, None] * p_im[:, :, :, None, :]
    cp_im = c_re[None, None] * p_im[:, :, :, None, :] + c_im[None, None] * p_re[:, :, :, None, :]
    kern = (jnp.einsum('tdgcp,dgpe->tdgce', cp_re[:c_len], bb_re)
            - jnp.einsum('tdgcp,dgpe->tdgce', cp_im[:c_len], bb_im))
    lag = jnp.arange(c_len)[None, :] - jnp.arange(c_len)[:, None]
    tz = jnp.where((lag >= 0)[:, :, None, None, None, None], kern[jnp.clip(lag, 0)], 0.0)
    tz = tz.transpose(2, 3, 0, 5, 1, 4).reshape(2 * S5_GROUPS, c_len * S5_GROUP, c_len * S5_GROUP)
    pr, pi = p_re[c_len - 1::-1][:c_len], p_im[c_len - 1::-1][:c_len]
    fm_re = pr[..., None] * bb_re[None] - pi[..., None] * bb_im[None]
    fm_im = pr[..., None] * bb_im[None] + pi[..., None] * bb_re[None]
    fm = jnp.concatenate([fm_re, fm_im], axis=3)
    fm = fm.transpose(1, 2, 0, 4, 3).reshape(2 * S5_GROUPS, c_len * S5_GROUP, 2 * S5_STATE)
    em = jnp.concatenate([cp_re[1:], -cp_im[1:]], axis=4)
    em = em.transpose(1, 2, 4, 0, 3).reshape(2 * S5_GROUPS, 2 * S5_STATE, c_len * S5_GROUP)
    ac_re = p_re[c_len].reshape(2 * S5_GROUPS, 1, S5_STATE)
    ac_im = p_im[c_len].reshape(2 * S5_GROUPS, 1, S5_STATE)
    a1 = jnp.concatenate([ac_re, ac_re], axis=2)
    a2 = jnp.concatenate([-ac_im, ac_im], axis=2)
    return tz.astype(BF16), fm.astype(BF16), em.astype(BF16), a1, a2


def _block_diag(n_blocks, size, value):
    return jnp.asarray(np.kron(np.eye(n_blocks), np.full((size, size), value)), BF16)


def kernel(x, c, ctx, c_ctx, w_mod, b_mod, norm1_g, norm2_g, w_in, gla_w_a2, gla_b_a, gla_norm_g, rwkv_conv, rwkv_w0, rwkv_w2, rwkv_a0, rwkv_a2, rwkv_g2, rwkv_k_k, rwkv_k_a, rwkv_r_k, rwkv_ln_w, rwkv_ln_b, s5_lam_re, s5_lam_im, s5_log_step, s5_b_re, s5_b_im, s5_c_re, s5_c_im, s5_d, s5_w_glu, w_branch, w_out, peer_w_q, peer_sub_keys, peer_u, peer_v, final_g):
    depth = w_in.shape[0]
    n_ctx, seq = ctx.shape[1], x.shape[1]
    ntok = n_ctx + seq
    n_ctx_tiles = n_ctx // TOKEN_TILE
    assert x.shape[0] == 1 and n_ctx % TOKEN_TILE == 0 and seq % (SCAN_CHUNKS * CHUNK) == 0 and seq % GRID_W == 0
    n_chunks = ntok // CHUNK

    xs = jnp.concatenate([ctx[0], x[0]], axis=0)
    cc = jnp.zeros((8, D_MODEL), F32).at[0].set(c[0]).at[1].set(c_ctx)
    mods = _mod_vectors(cc, w_mod, b_mod)

    bd_avg = _block_diag(RWKV_HEADS, RWKV_N, 1.0 / RWKV_N)
    bd_one = _block_diag(RWKV_HEADS, RWKV_N, 1.0)
    zrow = jnp.zeros((1, 2048), F32)

    for i in range(depth):
        mod = mods[i]
        w = w_in[i]
        zpad = lambda n: jnp.zeros((D_MODEL, n), F32)
        w_p = jnp.concatenate([
            w[:, 0:1024], w[:, 1024:1536], w[:, 3488:4000], w[:, 1568:3488], zpad(128),
            w[:, 4000:7072], w[:, 1536:1568], zpad(96)], axis=1).astype(BF16)
        z = _in_projection(xs, mod, norm1_g[i][None], w_p, n_ctx_tiles)

        zr = z[:, Z_RWKV:Z_RWKV + 2048]
        z_prev = jnp.concatenate([zrow, zr[:n_ctx - 1], zrow, zr[n_ctx:-1]], axis=0)
        z_next = jnp.concatenate([zr[1:n_ctx], zrow, zr[n_ctx + 1:], zrow], axis=0)
        conv = jnp.pad(rwkv_conv[i], ((0, 5), (0, 128)))
        w2p = jnp.stack([_pad_rows(rwkv_w2[i, d], 128, d * 64) for d in range(2)])
        a2p = jnp.stack([_pad_rows(rwkv_a2[i, d], 128, d * 64) for d in range(2)])
        gwp = jnp.stack([_pad_rows(gla_w_a2[i, d], 128, d * GLA_RANK) for d in range(2)])
        pc, pf, pb, ex, la = _mixer_prep(
            z, z_prev, z_next, conv, rwkv_w0[i], w2p, rwkv_a0[i], a2p, rwkv_k_k[i][None], rwkv_k_a[i][None],
            bd_one, gwp, gla_b_a[i])

        def gla_in(cols, width, heads):
            a = z[:, cols:cols + width]
            return jnp.concatenate([_to_heads(_scan_order(a, n_ctx, True, rev), heads) for rev in (False, True)])
        gq, gk, gv = gla_in(0, 256, GLA_HEADS), gla_in(256, 256, GLA_HEADS), gla_in(512, 512, GLA_HEADS)
        gla = jnp.concatenate([
            _to_heads(_scan_order(la[:, d * 256:(d + 1) * 256], n_ctx, True, bool(d)), GLA_HEADS) for d in range(2)])
        go = _gla_scan(gq, gk, gv, gla)
        o_f = _raster_order(_from_heads(go[:GLA_HEADS]), n_ctx, True, False)
        o_b = _raster_order(_from_heads(go[GLA_HEADS:]), n_ctx, True, True)

        def rw_in(a, cols):
            return [_to_heads(_scan_order(a[:, cols:cols + 512], n_ctx, False, rev), RWKV_HEADS) for rev in (False, True)]
        r_h, v_h, kk_h = (jnp.concatenate(rw_in(pc, o)) for o in (0, 512, 1024))
        lw_h, kd_h, bt_h = (jnp.concatenate([rw_in(pf, o)[0], rw_in(pb, o)[1]]) for o in (0, 512, 1024))
        ry = _rwkv_scan(r_h, v_h, kk_h, lw_h, kd_h, bt_h)
        y_f = _from_heads(ry[:RWKV_HEADS])
        y_b = _raster_order(_from_heads(ry[RWKV_HEADS:]), n_ctx, False, True)

        tz, fm, em, a1, a2 = _s5_operators(s5_lam_re[i], s5_lam_im[i], s5_log_step[i], s5_b_re[i], s5_b_im[i],
                                           s5_c_re[i], s5_c_im[i])
        su = z[:, Z_S5:Z_S5 + S5_WIDTH]

        def s5_in(rev):
            a = _scan_order(su, n_ctx, False, rev).reshape(n_chunks, CHUNK, S5_GROUPS, S5_GROUP)
            return a.transpose(2, 0, 1, 3).reshape(S5_GROUPS, n_chunks, CHUNK * S5_GROUP)
        sy = _s5_scan(jnp.concatenate([s5_in(False), s5_in(True)]), tz, fm, em, a1, a2)

        def s5_out(a, rev):
            a = a.reshape(S5_GROUPS, n_chunks, CHUNK, S5_GROUP).transpose(1, 2, 0, 3).reshape(ntok, S5_WIDTH)
            return _raster_order(a, n_ctx, False, rev)
        s_f, s_b = s5_out(sy[:S5_GROUPS], False), s5_out(sy[S5_GROUPS:], True)

        consts = (
            gla_norm_g[i][None], rwkv_ln_w[i][None], rwkv_ln_b[i][None], rwkv_r_k[i].reshape(1, RWKV_W),
            rwkv_g2[i].astype(BF16), s5_d[i][None], s5_w_glu[i].astype(BF16), w_branch[i].astype(BF16),
            w_out[i].astype(BF16), norm2_g[i][None], bd_avg, bd_one)
        xs, hl = _side_out(xs, z, o_f, o_b, y_f, y_b, pc, ex, s_f, s_b, mod, consts, n_ctx_tiles)

        wq_t = peer_w_q[i].T
        wq_hi = wq_t.astype(BF16)
        wq_lo = (wq_t - wq_hi.astype(F32)).astype(BF16)
        s1, s2, e1, e2, tau = _peer_scores(hl, wq_hi, wq_lo, peer_sub_keys[i])
        f_t = _peer_experts(hl.astype(BF16), s1, s2, e1, e2, tau, peer_u[i].astype(BF16), peer_v[i].T.astype(BF16))
        xs = _residual(xs, f_t.T, mod, final_g[None], n_ctx_tiles, final=(i == depth - 1))

    return xs[n_ctx:][None]
```

```python
import functools
import math

import jax
import jax.numpy as jnp
import numpy as np
from jax import lax
from jax.experimental import pallas as pl
from jax.experimental.pallas import tpu as pltpu

F32 = jnp.float32
BF16 = jnp.bfloat16

D_MODEL = 1024
GRID_W = 64
CHUNK = 64
TOKEN_TILE = 256
TILE_CHUNKS = TOKEN_TILE // CHUNK

GLA_HEADS, GLA_DK, GLA_DV, GLA_RANK, GLA_TAU = 4, 64, 128, 16, 16.0
RWKV_HEADS, RWKV_N, RWKV_W = 8, 64, 512
RWKV_PAIRS = RWKV_HEADS // 2
RWKV_GN_EPS = 64e-5
S5_WIDTH, S5_GROUP, S5_GROUPS, S5_STATE = 512, 16, 32, 64
PEER_HEADS, PEER_NKEYS, PEER_HALF, PEER_TOPK = 8, 128, 128, 16
PEER_EXPERT_BLOCK = 2048

Z_QKV, Z_GG, Z_S5, Z_RWKV, Z_GATE, Z_ALR, Z_WIDTH = 0, 1024, 1536, 2048, 4096, 7168, 8192
Z_NBLOCK = 2048
VMEM_LIMIT = 52 * 1024 * 1024

NN = (((1,), (0,)), ((), ()))
NT = (((1,), (1,)), ((), ()))
TN = (((0,), (0,)), ((), ()))


def _dg(a, b, dims):
    return lax.dot_general(a, b, dims, preferred_element_type=F32)


def _dot(a, b, dims=NN):
    return _dg(a.astype(BF16), b.astype(BF16), dims)


def _split2(a):
    hi = a.astype(BF16)
    lo = (a - hi.astype(F32)).astype(BF16)
    return hi, lo


def _split3(a):
    p1 = a.astype(BF16)
    r1 = a - p1.astype(F32)
    p2 = r1.astype(BF16)
    p3 = (r1 - p2.astype(F32)).astype(BF16)
    return p1, p2, p3


def _dot3(a, b, dims=NN):
    ah, al = _split2(a)
    bh, bl = _split2(b)
    return _dg(ah, bh, dims) + (_dg(ah, bl, dims) + _dg(al, bh, dims))


def _dot_exact_lhs(a_bf, b, dims=NN):
    b1, b2, b3 = _split3(b)
    return _dg(a_bf, b1, dims) + (_dg(a_bf, b2, dims) + _dg(a_bf, b3, dims))


def _dot_exact_rhs(a, b_bf, dims=NN):
    a1, a2, a3 = _split3(a)
    return _dg(a1, b_bf, dims) + (_dg(a2, b_bf, dims) + _dg(a3, b_bf, dims))


def _sigmoid(x):
    return 1.0 / (1.0 + jnp.exp(-x))


def _softplus(x):
    return jnp.maximum(x, 0.0) + jnp.log(1.0 + jnp.exp(-jnp.abs(x)))


def _gelu(x):
    c = math.sqrt(2.0 / math.pi)
    return x * (0.5 * (1.0 + jnp.tanh(c * (x + 0.044715 * (x * x * x)))))


def _rms(x, eps=1e-6):
    return x * lax.rsqrt(jnp.mean(x * x, axis=-1, keepdims=True) + eps)


def _params(*sem):
    return pltpu.CompilerParams(dimension_semantics=sem, vmem_limit_bytes=VMEM_LIMIT)


def _full(shape):
    n = len(shape)
    return pl.BlockSpec(shape, lambda *_: (0,) * n)


def _mod_body(c_ref, w_ref, b_ref, o_ref):
    c = c_ref[...]
    s = c * _sigmoid(c)
    o_ref[...] = _dot3(s, w_ref[...]) + b_ref[...]


def _mod_vectors(cc, w_mod, b_mod):
    depth, d, n = w_mod.shape
    nb = 1024
    return pl.pallas_call(
        _mod_body,
        grid=(depth, n // nb),
        in_specs=[
            pl.BlockSpec((8, d), lambda l, j: (0, 0)),
            pl.BlockSpec((None, d, nb), lambda l, j: (l, 0, j)),
            pl.BlockSpec((None, 1, nb), lambda l, j: (l, 0, j)),
        ],
        out_specs=pl.BlockSpec((None, 8, nb), lambda l, j: (l, 0, j)),
        out_shape=jax.ShapeDtypeStruct((depth, 8, n), F32),
        compiler_params=_params("parallel", "parallel"),
        name="mod_vectors",
    )(cc, w_mod, b_mod.reshape(depth, 1, n))


def _mod_row(mod_ref, tile_idx, n_lat_tiles, k):
    row = jnp.where(tile_idx >= n_lat_tiles, 1, 0)
    return mod_ref[pl.ds(row, 1), k * D_MODEL:(k + 1) * D_MODEL]


def _inproj_body(x_ref, mod_ref, g_ref, w_ref, o_ref, *, n_lat_tiles):
    i = pl.program_id(1)
    y = _rms(x_ref[...]) * g_ref[...]
    h = y * (1.0 + _mod_row(mod_ref, i, n_lat_tiles, 1)) + _mod_row(mod_ref, i, n_lat_tiles, 0)
    o_ref[...] = _dot(h, w_ref[...])


def _in_projection(x, mod, g, w_bf, n_lat_tiles):
    ntok, d = x.shape
    n = w_bf.shape[1]
    tm, nb = TOKEN_TILE, Z_NBLOCK
    return pl.pallas_call(
        functools.partial(_inproj_body, n_lat_tiles=n_lat_tiles),
        grid=(n // nb, ntok // tm),
        in_specs=[
            pl.BlockSpec((tm, d), lambda j, i: (i, 0)),
            _full(mod.shape),
            _full(g.shape),
            pl.BlockSpec((d, nb), lambda j, i: (0, j)),
        ],
        out_specs=pl.BlockSpec((tm, nb), lambda j, i: (i, j)),
        out_shape=jax.ShapeDtypeStruct((ntok, n), F32),
        compiler_params=_params("parallel", "parallel"),
        name="in_projection",
    )(x, mod, g, w_bf)


def _prep_body(zr_ref, zp_ref, zn_ref, alr_ref, conv_ref, w0_ref, w2_ref, a0_ref, a2_ref, kk_ref, ka_ref,
               bd_ref, gw_ref, gb_ref, pc_ref, pd_ref, ex_ref, la_ref, *, n_lat_tiles, n_tiles):
    i = pl.program_id(0)
    tm = zr_ref.shape[0]
    zc = zr_ref[...]
    seg_first = jnp.logical_or(i == 0, i == n_lat_tiles)
    seg_last = jnp.logical_or(i == n_lat_tiles - 1, i == n_tiles - 1)
    prev_row = jnp.where(seg_first, 0.0, zp_ref[7:8, :])
    next_row = jnp.where(seg_last, 0.0, zn_ref[0:1, :])
    rows = lax.broadcasted_iota(jnp.int32, (tm, 1), 0)
    z_prev = jnp.where(rows == 0, prev_row, pltpu.roll(zc, 1, 0))
    z_next = jnp.where(rows == tm - 1, next_row, pltpu.roll(zc, tm - 1, 0))
    conv = conv_ref[...]
    zs = conv[0:1] * z_prev + conv[1:2] * zc + conv[2:3] * z_next
    r = zs[:, 0:512]
    k = zs[:, 512:1024]
    v = zs[:, 1024:1536]
    wlr = jnp.tanh(zs[:, 1536:1664])
    alr = zs[:, 1664:1792]
    kk = k * kk_ref[...]
    nrm = jnp.sqrt(_dot_exact_rhs(kk * kk, bd_ref[...]))
    kk = kk / jnp.maximum(nrm, 1e-12)
    pc_ref[:, 0:512] = r
    pc_ref[:, 512:1024] = v
    pc_ref[:, 1024:1536] = kk
    kd_sum = jnp.zeros_like(k)
    for d in range(2):
        xw = w0_ref[d:d + 1, :] + _dot3(wlr, w2_ref[d])
        pd_ref[d, :, 0:512] = -jnp.exp(-_softplus(-xw) - 0.5)
        a = _sigmoid(a0_ref[d:d + 1, :] + _dot3(alr, a2_ref[d]))
        kd = k * (1.0 + (a - 1.0) * ka_ref[...])
        kd_sum = kd_sum + kd
        pd_ref[d, :, 512:1024] = kd
        pd_ref[d, :, 1024:1536] = kk * a
    ex_ref[:, 0:512] = kd_sum
    ex_ref[:, 512:640] = zs[:, 1792:1920]
    ga = alr_ref[...]
    for d in range(2):
        pre = _dot3(ga, gw_ref[d]) + gb_ref[d:d + 1, :]
        la_ref[:, d * 256:(d + 1) * 256] = -_softplus(-pre) * (1.0 / GLA_TAU)


def _mixer_prep(z, conv, w0, w2p, a0, a2p, k_k, k_a, bd_ones, gwp, gb, n_lat_tiles):
    ntok = z.shape[0]
    tm = TOKEN_TILE
    n_tiles = ntok // tm
    rw = 2048
    t8 = tm // 8
    tok = lambda w, j: pl.BlockSpec((tm, w), lambda i: (i, j))
    return pl.pallas_call(
        functools.partial(_prep_body, n_lat_tiles=n_lat_tiles, n_tiles=n_tiles),
        grid=(n_tiles,),
        in_specs=[
            tok(rw, Z_RWKV // rw),
            pl.BlockSpec((8, rw), lambda i: (jnp.maximum(i * t8 - 1, 0), Z_RWKV // rw)),
            pl.BlockSpec((8, rw), lambda i: (jnp.minimum((i + 1) * t8, n_tiles * t8 - 1), Z_RWKV // rw)),
            tok(128, Z_ALR // 128),
            _full(conv.shape), _full(w0.shape), _full(w2p.shape), _full(a0.shape), _full(a2p.shape),
            _full(k_k.shape), _full(k_a.shape), _full(bd_ones.shape), _full(gwp.shape), _full(gb.shape),
        ],
        out_specs=[tok(1536, 0), pl.BlockSpec((2, tm, 1536), lambda i: (0, i, 0)), tok(640, 0), tok(512, 0)],
        out_shape=[
            jax.ShapeDtypeStruct((ntok, 1536), F32),
            jax.ShapeDtypeStruct((2, ntok, 1536), F32),
            jax.ShapeDtypeStruct((ntok, 640), F32),
            jax.ShapeDtypeStruct((ntok, 512), F32),
        ],
        compiler_params=_params("parallel"),
        name="mixer_prep",
    )(z, z, z, z, conv, w0, w2p, a0, a2p, k_k, k_a, bd_ones, gwp, gb)


def _scan_masks(fwd):
    row = lax.broadcasted_iota(jnp.int32, (CHUNK, CHUNK), 0)
    col = lax.broadcasted_iota(jnp.int32, (CHUNK, CHUNK), 1)
    diff = jnp.where(fwd, row - col, col - row)
    return diff > 0, diff >= 0, row == col


def _scan_last(x, fwd):
    return jnp.where(fwd, x[CHUNK - 1:CHUNK, :], x[0:1, :])


def _gla_body(zl_ref, zc_ref, lal_ref, lac_ref, ol_ref, oc_ref, st_ref, *, n_cstep, n_chunk):
    fwd = pl.program_id(0) == 0
    i = pl.program_id(1)
    is_ctx = i < n_cstep

    @pl.when(i == 0)
    def _():
        st_ref[...] = jnp.zeros_like(st_ref)

    _, incl, _ = _scan_masks(fwd)
    tri = jnp.where(incl, 1.0, 0.0).astype(BF16)
    heads = range(GLA_HEADS)
    hs = [slice(h * GLA_DK, (h + 1) * GLA_DK) for h in heads]
    sls, q_in, k_in, k_end, dec, vs = [], [], [], [], [], []
    for c in range(n_chunk):
        start = pl.multiple_of(jnp.where(fwd, c, n_chunk - 1 - c) * CHUNK, CHUNK)
        sl = pl.ds(start, CHUNK)
        x = jnp.where(is_ctx, zc_ref[sl, :], zl_ref[sl, :])
        la = jnp.where(is_ctx, lac_ref[sl, :], lal_ref[sl, :])
        cum = _dot_exact_lhs(tri, la)
        last = _scan_last(cum, fwd)
        k = x[:, 256:512]
        sls.append(sl)
        q_in.append(x[:, 0:256] * (GLA_DK ** -0.5) * jnp.exp(cum))
        k_in.append(k * jnp.exp(-cum))
        k_end.append(k * jnp.exp(last - cum))
        dec.append(jnp.exp(last))
        vs.append([x[:, 512 + h * GLA_DV:512 + (h + 1) * GLA_DV] for h in heads])
    chunks = range(n_chunk)
    att = [[jnp.where(incl, _dot(q_in[c][:, hs[h]], k_in[c][:, hs[h]], NT), 0.0) for h in heads] for c in chunks]
    intra = [[_dot(att[c][h], vs[c][h]) for h in heads] for c in chunks]
    kv = [[_dot(vs[c][h], k_end[c][:, hs[h]], TN) for h in heads] for c in chunks]
    st = [st_ref[h] for h in heads]
    for c in chunks:
        outs = []
        for h in heads:
            outs.append(intra[c][h] + _dot(q_in[c][:, hs[h]], st[h], NT))
            st[h] = st[h] * dec[c][:, hs[h]] + kv[c][h]
        o = jnp.concatenate(outs, axis=1)
        ol_ref[sls[c], :] = o

        @pl.when(is_ctx)
        def _(o=o, sl=sls[c]):
            oc_ref[sl, :] = o
    for h in heads:
        st_ref[h] = st[h]


def _gla_scan(z, la, n_lat, n_ctx):
    ntok = z.shape[0]
    rows = n_lat // GRID_W
    n_cstep = n_ctx // rows
    n_steps = n_cstep + GRID_W
    zw, lw, ow = Z_WIDTH // 1024, 2, 1

    def piece(d, i):
        k = jnp.minimum(i, n_cstep - 1)
        return GRID_W + jnp.where(d == 0, k, n_cstep - 1 - k)

    def col(d, i):
        k = jnp.clip(i - n_cstep, 0, GRID_W - 1)
        return jnp.where(d == 0, k, GRID_W - 1 - k)

    ol, oc = pl.pallas_call(
        functools.partial(_gla_body, n_cstep=n_cstep, n_chunk=rows // CHUNK),
        grid=(2, n_steps),
        in_specs=[
            pl.BlockSpec((rows, 1024), lambda d, i: (0, col(d, i) * zw)),
            pl.BlockSpec((rows, 1024), lambda d, i: (piece(d, i), 0)),
            pl.BlockSpec((rows, 256), lambda d, i: (0, col(d, i) * lw + d)),
            pl.BlockSpec((rows, 256), lambda d, i: (piece(d, i), d)),
        ],
        out_specs=[
            pl.BlockSpec((None, rows, 512), lambda d, i: (d, 0, col(d, i) * ow)),
            pl.BlockSpec((None, rows, 512), lambda d, i: (d, piece(d, i) - GRID_W, 0)),
        ],
        out_shape=[
            jax.ShapeDtypeStruct((2, ntok // GRID_W, GRID_W * 512), F32),
            jax.ShapeDtypeStruct((2, n_ctx, 512), F32),
        ],
        scratch_shapes=[pltpu.VMEM((GLA_HEADS, GLA_DV, GLA_DK), F32)],
        compiler_params=_params("parallel", "arbitrary"),
        name="gla_scan",
    )(z.reshape(ntok // GRID_W, GRID_W * Z_WIDTH), z, la.reshape(ntok // GRID_W, GRID_W * 512), la)
    return ol.reshape(2, ntok, 512), oc


INV_BASE_LOG2 = 4


def _unit_triangular_inverses(mats, eye_f):
    row = lax.broadcasted_iota(jnp.int32, (CHUNK, CHUNK), 0)
    col = lax.broadcasted_iota(jnp.int32, (CHUNK, CHUNK), 1)
    same = lambda s: (row >> s) == (col >> s)
    aps = [jnp.where(same(INV_BASE_LOG2), a, 0.0) for a in mats]
    invs = [eye_f + ap for ap in aps]
    for _ in range(INV_BASE_LOG2 - 1):
        aps = [_dot(ap, ap) for ap in aps]
        invs = [inv + _dot(inv, ap) for inv, ap in zip(invs, aps)]
    for s in range(INV_BASE_LOG2, int(math.log2(CHUNK))):
        off = same(s + 1) & jnp.logical_not(same(s))
        tmp = [_dot(jnp.where(off, a, 0.0), inv) for a, inv in zip(mats, invs)]
        invs = [inv + _dot(inv, t) for inv, t in zip(invs, tmp)]
    return invs


def _rwkv_body(r_ref, v_ref, kk_ref, lw_ref, kd_ref, bt_ref, y_ref, t_ref):
    fwd = pl.program_id(0) < RWKV_PAIRS

    @pl.when(pl.program_id(1) == 0)
    def _():
        t_ref[...] = jnp.zeros_like(t_ref)

    n = RWKV_N
    strict, incl, eye = _scan_masks(fwd)
    tri = jnp.where(incl, 1.0, 0.0).astype(BF16)
    eye_f = jnp.where(eye, 1.0, 0.0)
    sls, al, bq, kq, rq, vv, g_end = [], [], [], [], [], [], []
    for c in range(TILE_CHUNKS):
        start = pl.multiple_of(jnp.where(fwd, c, TILE_CHUNKS - 1 - c) * CHUNK, CHUNK)
        sl = pl.ds(start, CHUNK)
        sls.append(sl)
        r2, v2, kk2, lw2, kd2, bt2 = (ref[sl, :] for ref in (r_ref, v_ref, kk_ref, lw_ref, kd_ref, bt_ref))
        cl = _dot_exact_lhs(tri, lw2)
        e_pos = jnp.exp(cl)
        e_neg = jnp.exp(-cl)
        al2 = -kk2 * jnp.exp(cl - lw2)
        bq2 = bt2 * e_neg
        kq2 = kd2 * e_neg
        rq2 = r2 * e_pos
        g_end2 = _scan_last(e_pos, fwd)
        for hh in range(2):
            hs = slice(hh * n, (hh + 1) * n)
            for lst, val in ((al, al2), (bq, bq2), (kq, kq2), (rq, rq2), (vv, v2), (g_end, g_end2)):
                lst.append(val[:, hs])
    items = range(2 * TILE_CHUNKS)
    quad = [_dot(jnp.concatenate([al[m], rq[m]], axis=0), jnp.concatenate([bq[m], kq[m]], axis=0), NT) for m in items]
    a_ab = [jnp.where(strict, quad[m][0:n, 0:n], 0.0) for m in items]
    a_ak = [jnp.where(strict, quad[m][0:n, n:2 * n], 0.0) for m in items]
    p_b = [jnp.where(incl, quad[m][n:2 * n, 0:n], 0.0) for m in items]
    p_k = [jnp.where(incl, quad[m][n:2 * n, n:2 * n], 0.0) for m in items]
    inv = _unit_triangular_inverses(a_ab, eye_f)
    akv = [_dot(a_ak[m], vv[m]) for m in items]
    sol = [_dot(inv[m], jnp.concatenate([al[m], akv[m]], axis=1)) for m in items]
    pbs = [_dot(p_b[m], sol[m]) for m in items]
    pkv = [_dot(p_k[m], vv[m]) for m in items]
    r_hat = [pbs[m][:, 0:n] + rq[m] for m in items]
    y0 = [pkv[m] + pbs[m][:, n:2 * n] for m in items]
    b_end = [bq[m] * g_end[m] for m in items]
    t_add = [_dot(jnp.concatenate([kq[m] * g_end[m], b_end[m]], axis=0),
                  jnp.concatenate([vv[m], sol[m][:, n:2 * n]], axis=0), TN) for m in items]
    g = [_dot(b_end[m], sol[m][:, 0:n], TN) + eye_f * g_end[m] for m in items]
    t = [t_ref[0], t_ref[1]]
    for c in range(TILE_CHUNKS):
        ys = []
        for hh in range(2):
            m = 2 * c + hh
            ys.append(y0[m] + _dot(r_hat[m], t[hh]))
            t[hh] = _dot3(g[m], t[hh]) + t_add[m]
        y_ref[sls[c], :] = jnp.concatenate(ys, axis=1)
    t_ref[0] = t[0]
    t_ref[1] = t[1]


def _rwkv_scan(pc, pd, n_lat_tiles, n_ctx_tiles):
    ntok = pc.shape[0]
    tm = TOKEN_TILE
    n_steps = n_lat_tiles + n_ctx_tiles
    hb = RWKV_W // 128

    def tile(ch, i):
        fwd = ch < RWKV_PAIRS
        kc = jnp.minimum(i, n_ctx_tiles - 1)
        kl = jnp.clip(i - n_ctx_tiles, 0, n_lat_tiles - 1)
        return jnp.where(i < n_ctx_tiles,
                         n_lat_tiles + jnp.where(fwd, kc, n_ctx_tiles - 1 - kc),
                         jnp.where(fwd, kl, n_lat_tiles - 1 - kl))

    com = lambda g: pl.BlockSpec((tm, 128), lambda ch, i: (tile(ch, i), g * hb + ch % RWKV_PAIRS))
    dirn = lambda g: pl.BlockSpec((None, tm, 128),
                                  lambda ch, i: (ch // RWKV_PAIRS, tile(ch, i), g * hb + ch % RWKV_PAIRS))
    return pl.pallas_call(
        _rwkv_body,
        grid=(2 * RWKV_PAIRS, n_steps),
        in_specs=[com(0), com(1), com(2), dirn(0), dirn(1), dirn(2)],
        out_specs=dirn(0),
        out_shape=jax.ShapeDtypeStruct((2, ntok, RWKV_W), F32),
        scratch_shapes=[pltpu.VMEM((2, RWKV_N, RWKV_N), F32)],
        compiler_params=_params("parallel", "arbitrary"),
        name="rwkv_scan",
    )(pc, pc, pc, pd, pd, pd)


def _s5_body(u_ref, tz_ref, fm_ref, em_ref, a1_ref, a2_ref, y_ref, f_scr, x_scr, *, n_chunks, n_lat_chunks):
    fwd = pl.program_id(0) < S5_GROUPS
    u = u_ref[...].astype(BF16)
    f_scr[...] = _dg(u, fm_ref[...], NN)
    a1 = a1_ref[...]
    a2 = a2_ref[...]

    def step(s, x):
        c = jnp.where(fwd, lax.rem(s + n_lat_chunks, n_chunks), n_chunks - 1 - s)
        x_scr[pl.ds(c, 1), :] = x
        return x * a1 + pltpu.roll(x, S5_STATE, 1) * a2 + f_scr[pl.ds(c, 1), :]

    lax.fori_loop(0, n_chunks, step, jnp.zeros((1, 2 * S5_STATE), F32))
    y_ref[...] = _dg(u, tz_ref[...], NN) + _dot(x_scr[...], em_ref[...])


def _s5_scan(u, tz, fm, em, a1, a2, n_lat_chunks):
    n_chunks = u.shape[0]
    n_groups, w = S5_GROUPS, S5_GROUP * CHUNK
    p2 = 2 * S5_STATE
    chain = lambda r, c: pl.BlockSpec((None, r, c), lambda b: (b, 0, 0))
    return pl.pallas_call(
        functools.partial(_s5_body, n_chunks=n_chunks, n_lat_chunks=n_lat_chunks),
        grid=(2 * n_groups,),
        in_specs=[
            pl.BlockSpec((n_chunks, w), lambda b: (0, b % n_groups)),
            chain(w, w), chain(w, p2), chain(p2, w), chain(1, p2), chain(1, p2),
        ],
        out_specs=pl.BlockSpec((None, n_chunks, w), lambda b: (b // n_groups, 0, b % n_groups)),
        out_shape=jax.ShapeDtypeStruct((2, n_chunks, n_groups * w), F32),
        scratch_shapes=[pltpu.VMEM((n_chunks, p2), F32), pltpu.VMEM((n_chunks, p2), F32)],
        compiler_params=_params("parallel"),
        name="s5_scan",
    )(u, tz, fm, em, a1, a2)


def _side_body(x_ref, zgg_ref, zu_ref, zg0_ref, zg1_ref, zg2_ref, olf_ref, olb_ref, ocf_ref, ocb_ref,
               yf_ref, yb_ref, r_ref, v_ref, ex_ref, sf_ref, sb_ref, mod_ref, gng_ref, lnw_ref, lnb_ref,
               rk_ref, g2_ref, s5d_ref, wglu_ref, wbr_ref, wout_ref, n2g_ref, bda_ref, bdo_ref,
               xo_ref, hl_ref, *, n_lat_tiles):
    i = pl.program_id(0)
    o = jnp.where(i >= n_lat_tiles, ocf_ref[...] + ocb_ref[...], olf_ref[...] + olb_ref[...])
    og = jnp.concatenate(
        [_rms(o[:, h * GLA_DV:(h + 1) * GLA_DV]) * gng_ref[...] for h in range(GLA_HEADS)], axis=1)
    zgg = zgg_ref[...]
    br_gla = og * (zgg * _sigmoid(zgg))
    y = yf_ref[...] + yb_ref[...]
    bda = bda_ref[...]
    mu = _dot_exact_rhs(y, bda)
    dy = y - mu
    var = _dot_exact_rhs(dy * dy, bda)
    yn = dy * lax.rsqrt(var + RWKV_GN_EPS) * lnw_ref[...] + lnb_ref[...]
    bonus = _dot_exact_rhs(r_ref[...] * ex_ref[:, 0:512] * rk_ref[...], bdo_ref[...])
    yn = yn + bonus * v_ref[...]
    br_rwkv = yn * _dot(_sigmoid(ex_ref[:, 512:640]), g2_ref[...])
    ys = sf_ref[...] + sb_ref[...] + s5d_ref[...] * zu_ref[...]
    lg = _dot(_gelu(ys), wglu_ref[...])
    br_s5 = lg[:, 0:S5_WIDTH] * _sigmoid(lg[:, S5_WIDTH:2 * S5_WIDTH])
    m = (_sigmoid(zg0_ref[...]) * _dot(br_gla, wbr_ref[0])
         + _sigmoid(zg1_ref[...]) * _dot(br_rwkv, wbr_ref[1])
         + _sigmoid(zg2_ref[...]) * _dot(br_s5, wbr_ref[2]))
    xn = x_ref[...] + _mod_row(mod_ref, i, n_lat_tiles, 2) * _dot(m, wout_ref[...])
    xo_ref[...] = xn
    hl_ref[...] = (_rms(xn) * n2g_ref[...] * (1.0 + _mod_row(mod_ref, i, n_lat_tiles, 4))
                   + _mod_row(mod_ref, i, n_lat_tiles, 3))


def _side_out(x, z, o_lat, o_ctx, y, pc, ex, sy, mod, consts, n_lat_tiles):
    ntok = x.shape[0]
    tm = TOKEN_TILE
    tok = lambda w, j: pl.BlockSpec((tm, w), lambda i: (i, j))
    dirn = lambda d: pl.BlockSpec((None, tm, 512), lambda i: (d, i, 0))
    lato = lambda d: pl.BlockSpec((None, tm, 512), lambda i: (d, jnp.minimum(i, n_lat_tiles - 1), 0))
    ctxo = lambda d: pl.BlockSpec((None, tm, 512), lambda i: (d, jnp.maximum(i - n_lat_tiles, 0), 0))
    return pl.pallas_call(
        functools.partial(_side_body, n_lat_tiles=n_lat_tiles),
        grid=(ntok // tm,),
        in_specs=[
            tok(D_MODEL, 0),
            tok(512, Z_GG // 512), tok(512, Z_S5 // 512),
            tok(1024, Z_GATE // 1024), tok(1024, Z_GATE // 1024 + 1), tok(1024, Z_GATE // 1024 + 2),
            lato(0), lato(1), ctxo(0), ctxo(1), dirn(0), dirn(1),
            tok(512, 0), tok(512, 1), tok(640, 0), dirn(0), dirn(1),
            _full(mod.shape),
        ] + [_full(c.shape) for c in consts],
        out_specs=[tok(D_MODEL, 0), tok(D_MODEL, 0)],
        out_shape=[jax.ShapeDtypeStruct((ntok, D_MODEL), F32)] * 2,
        compiler_params=_params("parallel"),
        name="side_out",
    )(x, z, z, z, z, z, o_lat, o_lat, o_ctx, o_ctx, y, y, pc, pc, ex, sy, sy, mod, *consts)


def _top_values(xs, k):
    xs = list(xs)
    outs = [[] for _ in xs]
    for _ in range(k):
        ms = [jnp.max(x, axis=0, keepdims=True) for x in xs]
        for o, m in zip(outs, ms):
            o.append(m)
        xs = [jnp.where(x == m, -jnp.inf, x) for x, m in zip(xs, ms)]
    return outs


PEER_HEAD_GROUP = 2


def _peer_score_body(h_ref, wh_ref, wl_ref, keys_ref, s1_ref, s2_ref, e1_ref, e2_ref, tau_ref):
    hh, hl = _split2(h_ref[...])
    for h0 in range(0, PEER_HEADS, PEER_HEAD_GROUP):
        group = range(h0, h0 + PEER_HEAD_GROUP)
        scores = []
        for hd in group:
            for s in range(2):
                rows = slice((hd * 2 + s) * PEER_HALF, (hd * 2 + s + 1) * PEER_HALF)
                wh, wl = wh_ref[rows, :], wl_ref[rows, :]
                q_t = _dg(wh, hh, NT) + (_dg(wh, hl, NT) + _dg(wl, hh, NT))
                scores.append(_dot3(keys_ref[hd, s], q_t))
        tops = _top_values(scores, PEER_TOPK)
        cands = []
        for n in range(PEER_HEAD_GROUP):
            t1, t2 = tops[2 * n], tops[2 * n + 1]
            t2_all = jnp.concatenate(t2, axis=0)
            cands.append(jnp.concatenate([t1[a] + t2_all for a in range(PEER_TOPK)], axis=0))
        bests = _top_values(cands, PEER_TOPK)
        for n, hd in enumerate(group):
            best = bests[n]
            zsum = jnp.ones_like(best[0])
            for b in best[1:]:
                zsum = zsum + jnp.exp(b - best[0])
            tau_ref[hd] = best[-1]
            s1_ref[hd] = scores[2 * n]
            s2_ref[hd] = scores[2 * n + 1]
            e1_ref[hd] = jnp.exp(scores[2 * n] - tops[2 * n][0])
            e2_ref[hd] = jnp.exp(scores[2 * n + 1] - tops[2 * n + 1][0]) / zsum


def _peer_scores(h, wq_hi, wq_lo, keys):
    ntok = h.shape[0]
    tm = TOKEN_TILE
    big = pl.BlockSpec((PEER_HEADS, PEER_NKEYS, tm), lambda i: (0, 0, i))
    big_shape = jax.ShapeDtypeStruct((PEER_HEADS, PEER_NKEYS, ntok), F32)
    return pl.pallas_call(
        _peer_score_body,
        grid=(ntok // tm,),
        in_specs=[pl.BlockSpec((tm, D_MODEL), lambda i: (i, 0)),
                  _full(wq_hi.shape), _full(wq_lo.shape), _full(keys.shape)],
        out_specs=[big, big, big, big, pl.BlockSpec((PEER_HEADS, 1, tm), lambda i: (0, 0, i))],
        out_shape=[big_shape] * 4 + [jax.ShapeDtypeStruct((PEER_HEADS, 1, ntok), F32)],
        compiler_params=_params("parallel"),
        name="peer_scores",
    )(h, wq_hi, wq_lo, keys)


def _peer_expert_body(h_ref, s1_ref, s2_ref, e1_ref, e2_ref, tau_ref, u_ref, vt_ref, o_ref, g_scr):
    j = pl.program_id(1)

    @pl.when(j == 0)
    def _():
        o_ref[...] = jnp.zeros_like(o_ref)

    n_i1 = PEER_EXPERT_BLOCK // PEER_NKEYS
    for il in range(n_i1):
        i1 = j * n_i1 + il
        acc = None
        for hd in range(PEER_HEADS):
            sel = (s2_ref[hd] + s1_ref[hd, pl.ds(i1, 1), :]) >= tau_ref[hd]
            w = jnp.where(sel, e2_ref[hd] * e1_ref[hd, pl.ds(i1, 1), :], 0.0)
            acc = w if acc is None else acc + w
        g_scr[il * PEER_NKEYS:(il + 1) * PEER_NKEYS, :] = acc
    act = _gelu(_dg(u_ref[...], h_ref[...], NT))
    o_ref[...] += _dg(vt_ref[...], (g_scr[...] * act).astype(BF16), NN)


def _peer_experts(h_bf, s1, s2, e1, e2, tau, u_bf, vt_bf):
    ntok = h_bf.shape[0]
    n_exp = u_bf.shape[0]
    tm, eb = TOKEN_TILE, PEER_EXPERT_BLOCK
    big = pl.BlockSpec((PEER_HEADS, PEER_NKEYS, tm), lambda i, j: (0, 0, i))
    return pl.pallas_call(
        _peer_expert_body,
        grid=(ntok // tm, n_exp // eb),
        in_specs=[
            pl.BlockSpec((tm, D_MODEL), lambda i, j: (i, 0)),
            big, big, big, big,
            pl.BlockSpec((PEER_HEADS, 1, tm), lambda i, j: (0, 0, i)),
            pl.BlockSpec((eb, D_MODEL), lambda i, j: (j, 0)),
            pl.BlockSpec((D_MODEL, eb), lambda i, j: (0, j)),
        ],
        out_specs=pl.BlockSpec((D_MODEL, tm), lambda i, j: (0, i)),
        out_shape=jax.ShapeDtypeStruct((D_MODEL, ntok), F32),
        scratch_shapes=[pltpu.VMEM((eb, tm), F32)],
        compiler_params=_params("parallel", "arbitrary"),
        name="peer_experts",
    )(h_bf, s1, s2, e1, e2, tau, u_bf, vt_bf)


def _residual_body(x_ref, f_ref, mod_ref, g_ref, o_ref, *, n_lat_tiles, final):
    xn = x_ref[...] + _mod_row(mod_ref, pl.program_id(0), n_lat_tiles, 5) * f_ref[...]
    if final:
        xn = _rms(xn) * g_ref[...]
    o_ref[...] = xn


def _residual(x, f, mod, g, n_lat_tiles, final):
    ntok = x.shape[0]
    tm = TOKEN_TILE
    tok = pl.BlockSpec((tm, D_MODEL), lambda i: (i, 0))
    return pl.pallas_call(
        functools.partial(_residual_body, n_lat_tiles=n_lat_tiles, final=final),
        grid=(ntok // tm,),
        in_specs=[tok, tok, _full(mod.shape), _full(g.shape)],
        out_specs=tok,
        out_shape=jax.ShapeDtypeStruct((ntok, D_MODEL), F32),
        compiler_params=_params("parallel"),
        name="residual",
    )(x, f, mod, g)


def _pad_rows(w, rows, offset):
    return jnp.zeros((rows, w.shape[1]), w.dtype).at[offset:offset + w.shape[0]].set(w)


def _s5_operators(lam_re, lam_im, log_step, b_re, b_im, c_re, c_im):
    n, g, p, ch = CHUNK, S5_GROUPS, S5_STATE, S5_GROUP
    dt = jnp.exp(log_step)[..., None]
    mag = jnp.exp(lam_re * dt)
    a_re, a_im = mag * jnp.cos(lam_im * dt), mag * jnp.sin(lam_im * dt)
    den = lam_re * lam_re + lam_im * lam_im
    f_re = ((a_re - 1.0) * lam_re + a_im * lam_im) / den
    f_im = (a_im * lam_re - (a_re - 1.0) * lam_im) / den
    bb_re = f_re[..., None] * b_re - f_im[..., None] * b_im
    bb_im = f_re[..., None] * b_im + f_im[..., None] * b_re
    tau = jnp.arange(n + 1, dtype=F32)[:, None, None, None]
    pmag = jnp.exp(lam_re * dt * tau)
    p_re, p_im = pmag * jnp.cos(lam_im * dt * tau), pmag * jnp.sin(lam_im * dt * tau)
    cp_re = c_re[None, None] * p_re[:, :, :, None, :] - c_im[None, None] * p_im[:, :, :, None, :]
    cp_im = c_re[None, None] * p_im[:, :, :, None, :] + c_im[None, None] * p_re[:, :, :, None, :]
    kern = (jnp.einsum('tdgcp,dgpe->dgect', cp_re[:n], bb_re)
            - jnp.einsum('tdgcp,dgpe->dgect', cp_im[:n], bb_im)).astype(BF16)
    zeros = jnp.zeros_like(kern)
    pos = jnp.arange(n)
    tz, fm, em = [], [], []
    for d in range(2):
        if d == 0:
            padded = jnp.concatenate([zeros[d], kern[d]], axis=-1)
            rows = [padded[..., n - j:2 * n - j] for j in range(n)]
        else:
            padded = jnp.concatenate([kern[d, ..., ::-1], zeros[d]], axis=-1)
            rows = [padded[..., n - 1 - j:2 * n - 1 - j] for j in range(n)]
        t = jnp.stack(rows, axis=2)
        tz.append(t.reshape(g, ch * n, ch * n))
        pw = (n - 1 - pos) if d == 0 else pos
        pr, pi = p_re[pw, d], p_im[pw, d]
        f_r = pr[..., None] * bb_re[d][None] - pi[..., None] * bb_im[d][None]
        f_i = pr[..., None] * bb_im[d][None] + pi[..., None] * bb_re[d][None]
        f = jnp.concatenate([f_r, f_i], axis=2)
        fm.append(f.transpose(1, 3, 0, 2).reshape(g, ch * n, 2 * p))
        pw = (pos + 1) if d == 0 else (n - pos)
        e = jnp.concatenate([cp_re[pw, d], -cp_im[pw, d]], axis=3)
        em.append(e.transpose(1, 3, 2, 0).reshape(g, 2 * p, ch * n))
    ac_re = p_re[n].reshape(2 * g, 1, p)
    ac_im = p_im[n].reshape(2 * g, 1, p)
    a1 = jnp.concatenate([ac_re, ac_re], axis=2)
    a2 = jnp.concatenate([-ac_im, ac_im], axis=2)
    cat = lambda parts: jnp.concatenate(parts, axis=0).astype(BF16)
    return cat(tz), cat(fm), cat(em), a1, a2


def _block_diag(n_blocks, size, value):
    return jnp.asarray(np.kron(np.eye(n_blocks), np.full((size, size), value)), BF16)


def kernel(x, c, ctx, c_ctx, w_mod, b_mod, norm1_g, norm2_g, w_in, gla_w_a2, gla_b_a, gla_norm_g, rwkv_conv, rwkv_w0, rwkv_w2, rwkv_a0, rwkv_a2, rwkv_g2, rwkv_k_k, rwkv_k_a, rwkv_r_k, rwkv_ln_w, rwkv_ln_b, s5_lam_re, s5_lam_im, s5_log_step, s5_b_re, s5_b_im, s5_c_re, s5_c_im, s5_d, s5_w_glu, w_branch, w_out, peer_w_q, peer_sub_keys, peer_u, peer_v, final_g):
    depth = w_in.shape[0]
    n_ctx, seq = ctx.shape[1], x.shape[1]
    ntok = seq + n_ctx
    rows = seq // GRID_W
    assert x.shape[0] == 1 and seq % (GRID_W * CHUNK) == 0 and n_ctx % rows == 0
    assert seq % TOKEN_TILE == 0 and n_ctx % TOKEN_TILE == 0
    n_lat_tiles, n_ctx_tiles = seq // TOKEN_TILE, n_ctx // TOKEN_TILE
    n_chunks = ntok // CHUNK

    xs = jnp.concatenate([x[0], ctx[0]], axis=0)
    cc = jnp.zeros((8, D_MODEL), F32).at[0].set(c[0]).at[1].set(c_ctx)
    mods = _mod_vectors(cc, w_mod, b_mod)

    bd_avg = _block_diag(RWKV_HEADS, RWKV_N, 1.0 / RWKV_N)
    bd_one = _block_diag(RWKV_HEADS, RWKV_N, 1.0)

    for i in range(depth):
        mod = mods[i]
        w = w_in[i]
        zpad = lambda n: jnp.zeros((D_MODEL, n), F32)
        w_p = jnp.concatenate([
            w[:, 0:1024], w[:, 1024:1536], w[:, 3488:4000], w[:, 1568:3488], zpad(128),
            w[:, 4000:7072], w[:, 1536:1568], zpad(Z_WIDTH - Z_ALR - 32)], axis=1).astype(BF16)
        z = _in_projection(xs, mod, norm1_g[i][None], w_p, n_lat_tiles)

        conv = jnp.pad(rwkv_conv[i], ((0, 5), (0, 128)))
        w2p = jnp.stack([_pad_rows(rwkv_w2[i, d], 128, d * 64) for d in range(2)])
        a2p = jnp.stack([_pad_rows(rwkv_a2[i, d], 128, d * 64) for d in range(2)])
        gwp = jnp.stack([_pad_rows(gla_w_a2[i, d], 128, d * GLA_RANK) for d in range(2)])
        pc, pd, ex, la = _mixer_prep(z, conv, rwkv_w0[i], w2p, rwkv_a0[i], a2p, rwkv_k_k[i][None],
                                     rwkv_k_a[i][None], bd_one, gwp, gla_b_a[i], n_lat_tiles)

        o_lat, o_ctx = _gla_scan(z, la, seq, n_ctx)
        y = _rwkv_scan(pc, pd, n_lat_tiles, n_ctx_tiles)

        tz, fm, em, a1, a2 = _s5_operators(s5_lam_re[i], s5_lam_im[i], s5_log_step[i], s5_b_re[i], s5_b_im[i],
                                           s5_c_re[i], s5_c_im[i])
        su = z[:, Z_S5:Z_S5 + S5_WIDTH].reshape(n_chunks, CHUNK, S5_WIDTH).transpose(0, 2, 1)
        sy = _s5_scan(su.reshape(n_chunks, S5_WIDTH * CHUNK), tz, fm, em, a1, a2, seq // CHUNK)
        sy = sy.reshape(2, n_chunks, S5_WIDTH, CHUNK).transpose(0, 1, 3, 2).reshape(2, ntok, S5_WIDTH)

        consts = (
            gla_norm_g[i][None], rwkv_ln_w[i][None], rwkv_ln_b[i][None], rwkv_r_k[i].reshape(1, RWKV_W),
            rwkv_g2[i].astype(BF16), s5_d[i][None], s5_w_glu[i].astype(BF16), w_branch[i].astype(BF16),
            w_out[i].astype(BF16), norm2_g[i][None], bd_avg, bd_one)
        xs, hl = _side_out(xs, z, o_lat, o_ctx, y, pc, ex, sy, mod, consts, n_lat_tiles)

        wq_t = peer_w_q[i].T
        wq_hi = wq_t.astype(BF16)
        wq_lo = (wq_t - wq_hi.astype(F32)).astype(BF16)
        s1, s2, e1, e2, tau = _peer_scores(hl, wq_hi, wq_lo, peer_sub_keys[i])
        f_t = _peer_experts(hl.astype(BF16), s1, s2, e1, e2, tau, peer_u[i].astype(BF16), peer_v[i].T.astype(BF16))
        xs = _residual(xs, f_t.T, mod, final_g[None], n_lat_tiles, final=(i == depth - 1))

    return xs[:seq][None]
```

```python
import functools
import math

import jax
import jax.numpy as jnp
import numpy as np
from jax import lax
from jax.experimental import pallas as pl
from jax.experimental.pallas import tpu as pltpu

F32 = jnp.float32
BF16 = jnp.bfloat16

D_MODEL = 1024
GRID_W = 64
CHUNK = 64
TOKEN_TILE = 256
TILE_CHUNKS = TOKEN_TILE // CHUNK

GLA_HEADS, GLA_DK, GLA_DV, GLA_RANK, GLA_TAU = 4, 64, 128, 16, 16.0
RWKV_HEADS, RWKV_N, RWKV_W = 8, 64, 512
RWKV_PAIRS = RWKV_HEADS // 2
RWKV_GN_EPS = 64e-5
S5_WIDTH, S5_GROUP, S5_GROUPS, S5_STATE = 512, 16, 32, 64
PEER_HEADS, PEER_NKEYS, PEER_HALF, PEER_TOPK = 8, 128, 128, 16
PEER_EXPERT_BLOCK = 2048

Z_QKV, Z_GG, Z_S5, Z_RWKV, Z_GATE, Z_ALR, Z_WIDTH = 0, 1024, 1536, 2048, 4096, 7168, 8192
Z_NBLOCK = 2048
VMEM_LIMIT = 52 * 1024 * 1024

NN = (((1,), (0,)), ((), ()))
NT = (((1,), (1,)), ((), ()))
TN = (((0,), (0,)), ((), ()))


def _dg(a, b, dims):
    return lax.dot_general(a, b, dims, preferred_element_type=F32)


def _dot(a, b, dims=NN):
    return _dg(a.astype(BF16), b.astype(BF16), dims)


def _split2(a):
    hi = a.astype(BF16)
    lo = (a - hi.astype(F32)).astype(BF16)
    return hi, lo


def _split3(a):
    p1 = a.astype(BF16)
    r1 = a - p1.astype(F32)
    p2 = r1.astype(BF16)
    p3 = (r1 - p2.astype(F32)).astype(BF16)
    return p1, p2, p3


def _dot3(a, b, dims=NN):
    ah, al = _split2(a)
    bh, bl = _split2(b)
    return _dg(ah, bh, dims) + (_dg(ah, bl, dims) + _dg(al, bh, dims))


def _dot_exact_lhs(a_bf, b, dims=NN):
    b1, b2, b3 = _split3(b)
    return _dg(a_bf, b1, dims) + (_dg(a_bf, b2, dims) + _dg(a_bf, b3, dims))


def _dot_exact_rhs(a, b_bf, dims=NN):
    a1, a2, a3 = _split3(a)
    return _dg(a1, b_bf, dims) + (_dg(a2, b_bf, dims) + _dg(a3, b_bf, dims))


def _sigmoid(x):
    return 1.0 / (1.0 + jnp.exp(-x))


def _softplus(x):
    return jnp.maximum(x, 0.0) + jnp.log(1.0 + jnp.exp(-jnp.abs(x)))


def _gelu(x):
    c = math.sqrt(2.0 / math.pi)
    return x * (0.5 * (1.0 + jnp.tanh(c * (x + 0.044715 * (x * x * x)))))


def _rms(x, eps=1e-6):
    return x * lax.rsqrt(jnp.mean(x * x, axis=-1, keepdims=True) + eps)


def _params(*sem):
    return pltpu.CompilerParams(dimension_semantics=sem, vmem_limit_bytes=VMEM_LIMIT)


def _full(shape):
    n = len(shape)
    return pl.BlockSpec(shape, lambda *_: (0,) * n)


def _mod_body(c_ref, w_ref, b_ref, o_ref):
    c = c_ref[...]
    s = c * _sigmoid(c)
    o_ref[...] = _dot3(s, w_ref[...]) + b_ref[...]


def _mod_vectors(cc, w_mod, b_mod):
    depth, d, n = w_mod.shape
    nb = 1024
    return pl.pallas_call(
        _mod_body,
        grid=(depth, n // nb),
        in_specs=[
            pl.BlockSpec((8, d), lambda l, j: (0, 0)),
            pl.BlockSpec((None, d, nb), lambda l, j: (l, 0, j)),
            pl.BlockSpec((None, 1, nb), lambda l, j: (l, 0, j)),
        ],
        out_specs=pl.BlockSpec((None, 8, nb), lambda l, j: (l, 0, j)),
        out_shape=jax.ShapeDtypeStruct((depth, 8, n), F32),
        compiler_params=_params("parallel", "parallel"),
        name="mod_vectors",
    )(cc, w_mod, b_mod.reshape(depth, 1, n))


def _mod_row(mod_ref, tile_idx, n_lat_tiles, k):
    row = jnp.where(tile_idx >= n_lat_tiles, 1, 0)
    return mod_ref[pl.ds(row, 1), k * D_MODEL:(k + 1) * D_MODEL]


def _inproj_body(x_ref, mod_ref, g_ref, w_ref, o_ref, *, n_lat_tiles):
    i = pl.program_id(1)
    y = _rms(x_ref[...]) * g_ref[...]
    h = y * (1.0 + _mod_row(mod_ref, i, n_lat_tiles, 1)) + _mod_row(mod_ref, i, n_lat_tiles, 0)
    o_ref[...] = _dot(h, w_ref[...])


def _in_projection(x, mod, g, w_bf, n_lat_tiles):
    ntok, d = x.shape
    n = w_bf.shape[1]
    tm, nb = TOKEN_TILE, Z_NBLOCK
    return pl.pallas_call(
        functools.partial(_inproj_body, n_lat_tiles=n_lat_tiles),
        grid=(n // nb, ntok // tm),
        in_specs=[
            pl.BlockSpec((tm, d), lambda j, i: (i, 0)),
            _full(mod.shape),
            _full(g.shape),
            pl.BlockSpec((d, nb), lambda j, i: (0, j)),
        ],
        out_specs=pl.BlockSpec((tm, nb), lambda j, i: (i, j)),
        out_shape=jax.ShapeDtypeStruct((ntok, n), F32),
        compiler_params=_params("parallel", "parallel"),
        name="in_projection",
    )(x, mod, g, w_bf)


def _prep_body(zr_ref, zp_ref, zn_ref, alr_ref, conv_ref, w0_ref, w2_ref, a0_ref, a2_ref, kk_ref, ka_ref,
               bd_ref, gw_ref, gb_ref, pc_ref, pd_ref, ex_ref, la_ref, *, n_lat_tiles, n_tiles):
    i = pl.program_id(0)
    tm = zr_ref.shape[0]
    zc = zr_ref[...]
    seg_first = jnp.logical_or(i == 0, i == n_lat_tiles)
    seg_last = jnp.logical_or(i == n_lat_tiles - 1, i == n_tiles - 1)
    prev_row = jnp.where(seg_first, 0.0, zp_ref[7:8, :])
    next_row = jnp.where(seg_last, 0.0, zn_ref[0:1, :])
    rows = lax.broadcasted_iota(jnp.int32, (tm, 1), 0)
    z_prev = jnp.where(rows == 0, prev_row, pltpu.roll(zc, 1, 0))
    z_next = jnp.where(rows == tm - 1, next_row, pltpu.roll(zc, tm - 1, 0))
    conv = conv_ref[...]
    zs = conv[0:1] * z_prev + conv[1:2] * zc + conv[2:3] * z_next
    r = zs[:, 0:512]
    k = zs[:, 512:1024]
    v = zs[:, 1024:1536]
    wlr = jnp.tanh(zs[:, 1536:1664])
    alr = zs[:, 1664:1792]
    kk = k * kk_ref[...]
    nrm = jnp.sqrt(_dot_exact_rhs(kk * kk, bd_ref[...]))
    kk = kk / jnp.maximum(nrm, 1e-12)
    pc_ref[:, 0:512] = r
    pc_ref[:, 512:1024] = v
    pc_ref[:, 1024:1536] = kk
    kd_sum = jnp.zeros_like(k)
    for d in range(2):
        xw = w0_ref[d:d + 1, :] + _dot3(wlr, w2_ref[d])
        pd_ref[d, :, 0:512] = -jnp.exp(-_softplus(-xw) - 0.5)
        a = _sigmoid(a0_ref[d:d + 1, :] + _dot3(alr, a2_ref[d]))
        kd = k * (1.0 + (a - 1.0) * ka_ref[...])
        kd_sum = kd_sum + kd
        pd_ref[d, :, 512:1024] = kd
        pd_ref[d, :, 1024:1536] = kk * a
    ex_ref[:, 0:512] = kd_sum
    ex_ref[:, 512:640] = zs[:, 1792:1920]
    ga = alr_ref[...]
    for d in range(2):
        pre = _dot3(ga, gw_ref[d]) + gb_ref[d:d + 1, :]
        la_ref[:, d * 256:(d + 1) * 256] = -_softplus(-pre) * (1.0 / GLA_TAU)


def _mixer_prep(z, conv, w0, w2p, a0, a2p, k_k, k_a, bd_ones, gwp, gb, n_lat_tiles):
    ntok = z.shape[0]
    tm = TOKEN_TILE
    n_tiles = ntok // tm
    rw = 2048
    t8 = tm // 8
    tok = lambda w, j: pl.BlockSpec((tm, w), lambda i: (i, j))
    return pl.pallas_call(
        functools.partial(_prep_body, n_lat_tiles=n_lat_tiles, n_tiles=n_tiles),
        grid=(n_tiles,),
        in_specs=[
            tok(rw, Z_RWKV // rw),
            pl.BlockSpec((8, rw), lambda i: (jnp.maximum(i * t8 - 1, 0), Z_RWKV // rw)),
            pl.BlockSpec((8, rw), lambda i: (jnp.minimum((i + 1) * t8, n_tiles * t8 - 1), Z_RWKV // rw)),
            tok(128, Z_ALR // 128),
            _full(conv.shape), _full(w0.shape), _full(w2p.shape), _full(a0.shape), _full(a2p.shape),
            _full(k_k.shape), _full(k_a.shape), _full(bd_ones.shape), _full(gwp.shape), _full(gb.shape),
        ],
        out_specs=[tok(1536, 0), pl.BlockSpec((2, tm, 1536), lambda i: (0, i, 0)), tok(640, 0), tok(512, 0)],
        out_shape=[
            jax.ShapeDtypeStruct((ntok, 1536), F32),
            jax.ShapeDtypeStruct((2, ntok, 1536), F32),
            jax.ShapeDtypeStruct((ntok, 640), F32),
            jax.ShapeDtypeStruct((ntok, 512), F32),
        ],
        compiler_params=_params("parallel"),
        name="mixer_prep",
    )(z, z, z, z, conv, w0, w2p, a0, a2p, k_k, k_a, bd_ones, gwp, gb)


def _scan_masks(fwd):
    row = lax.broadcasted_iota(jnp.int32, (CHUNK, CHUNK), 0)
    col = lax.broadcasted_iota(jnp.int32, (CHUNK, CHUNK), 1)
    diff = jnp.where(fwd, row - col, col - row)
    return diff > 0, diff >= 0, row == col


def _scan_last(x, fwd):
    return jnp.where(fwd, x[CHUNK - 1:CHUNK, :], x[0:1, :])


def _gla_body(zl_ref, zc_ref, lal_ref, lac_ref, ol_ref, oc_ref, st_ref, *, n_cstep, n_chunk):
    fwd = pl.program_id(0) == 0
    i = pl.program_id(1)
    is_ctx = i < n_cstep

    @pl.when(i == 0)
    def _():
        st_ref[...] = jnp.zeros_like(st_ref)

    _, incl, _ = _scan_masks(fwd)
    tri = jnp.where(incl, 1.0, 0.0).astype(BF16)
    heads = range(GLA_HEADS)
    hs = [slice(h * GLA_DK, (h + 1) * GLA_DK) for h in heads]
    sls, q_in, k_in, k_end, dec, vs = [], [], [], [], [], []
    for c in range(n_chunk):
        start = pl.multiple_of(jnp.where(fwd, c, n_chunk - 1 - c) * CHUNK, CHUNK)
        sl = pl.ds(start, CHUNK)
        x = jnp.where(is_ctx, zc_ref[sl, :], zl_ref[sl, :])
        la = jnp.where(is_ctx, lac_ref[sl, :], lal_ref[sl, :])
        cum = _dot_exact_lhs(tri, la)
        last = _scan_last(cum, fwd)
        k = x[:, 256:512]
        sls.append(sl)
        q_in.append(x[:, 0:256] * (GLA_DK ** -0.5) * jnp.exp(cum))
        k_in.append(k * jnp.exp(-cum))
        k_end.append(k * jnp.exp(last - cum))
        dec.append(jnp.exp(last))
        vs.append([x[:, 512 + h * GLA_DV:512 + (h + 1) * GLA_DV] for h in heads])
    chunks = range(n_chunk)
    att = [[jnp.where(incl, _dot(q_in[c][:, hs[h]], k_in[c][:, hs[h]], NT), 0.0) for h in heads] for c in chunks]
    intra = [[_dot(att[c][h], vs[c][h]) for h in heads] for c in chunks]
    kv = [[_dot(vs[c][h], k_end[c][:, hs[h]], TN) for h in heads] for c in chunks]
    st = [st_ref[h] for h in heads]
    for c in chunks:
        outs = []
        for h in heads:
            outs.append(intra[c][h] + _dot(q_in[c][:, hs[h]], st[h], NT))
            st[h] = st[h] * dec[c][:, hs[h]] + kv[c][h]
        o = jnp.concatenate(outs, axis=1)
        ol_ref[sls[c], :] = o

        @pl.when(is_ctx)
        def _(o=o, sl=sls[c]):
            oc_ref[sl, :] = o
    for h in heads:
        st_ref[h] = st[h]


def _gla_scan(z, la, n_lat, n_ctx):
    ntok = z.shape[0]
    rows = n_lat // GRID_W
    n_cstep = n_ctx // rows
    n_steps = n_cstep + GRID_W
    zw, lw, ow = Z_WIDTH // 1024, 2, 1

    def piece(d, i):
        k = jnp.minimum(i, n_cstep - 1)
        return GRID_W + jnp.where(d == 0, k, n_cstep - 1 - k)

    def col(d, i):
        k = jnp.clip(i - n_cstep, 0, GRID_W - 1)
        return jnp.where(d == 0, k, GRID_W - 1 - k)

    ol, oc = pl.pallas_call(
        functools.partial(_gla_body, n_cstep=n_cstep, n_chunk=rows // CHUNK),
        grid=(2, n_steps),
        in_specs=[
            pl.BlockSpec((rows, 1024), lambda d, i: (0, col(d, i) * zw)),
            pl.BlockSpec((rows, 1024), lambda d, i: (piece(d, i), 0)),
            pl.BlockSpec((rows, 256), lambda d, i: (0, col(d, i) * lw + d)),
            pl.BlockSpec((rows, 256), lambda d, i: (piece(d, i), d)),
        ],
        out_specs=[
            pl.BlockSpec((None, rows, 512), lambda d, i: (d, 0, col(d, i) * ow)),
            pl.BlockSpec((None, rows, 512), lambda d, i: (d, piece(d, i) - GRID_W, 0)),
        ],
        out_shape=[
            jax.ShapeDtypeStruct((2, ntok // GRID_W, GRID_W * 512), F32),
            jax.ShapeDtypeStruct((2, n_ctx, 512), F32),
        ],
        scratch_shapes=[pltpu.VMEM((GLA_HEADS, GLA_DV, GLA_DK), F32)],
        compiler_params=_params("parallel", "arbitrary"),
        name="gla_scan",
    )(z.reshape(ntok // GRID_W, GRID_W * Z_WIDTH), z, la.reshape(ntok // GRID_W, GRID_W * 512), la)
    return ol.reshape(2, ntok, 512), oc


INV_BASE_LOG2 = 4


def _unit_triangular_inverses(mats, eye_f):
    row = lax.broadcasted_iota(jnp.int32, (CHUNK, CHUNK), 0)
    col = lax.broadcasted_iota(jnp.int32, (CHUNK, CHUNK), 1)
    same = lambda s: (row >> s) == (col >> s)
    aps = [jnp.where(same(INV_BASE_LOG2), a, 0.0) for a in mats]
    invs = [eye_f + ap for ap in aps]
    for _ in range(INV_BASE_LOG2 - 1):
        aps = [_dot(ap, ap) for ap in aps]
        invs = [inv + _dot(inv, ap) for inv, ap in zip(invs, aps)]
    for s in range(INV_BASE_LOG2, int(math.log2(CHUNK))):
        off = same(s + 1) & jnp.logical_not(same(s))
        tmp = [_dot(jnp.where(off, a, 0.0), inv) for a, inv in zip(mats, invs)]
        invs = [inv + _dot(inv, t) for inv, t in zip(invs, tmp)]
    return invs


def _rwkv_body(r_ref, v_ref, kk_ref, lw_ref, kd_ref, bt_ref, y_ref, t_ref):
    fwd = pl.program_id(0) < RWKV_PAIRS

    @pl.when(pl.program_id(1) == 0)
    def _():
        t_ref[...] = jnp.zeros_like(t_ref)

    n = RWKV_N
    strict, incl, eye = _scan_masks(fwd)
    tri = jnp.where(incl, 1.0, 0.0).astype(BF16)
    eye_f = jnp.where(eye, 1.0, 0.0)
    sls, al, bq, kq, rq, vv, g_end = [], [], [], [], [], [], []
    for c in range(TILE_CHUNKS):
        start = pl.multiple_of(jnp.where(fwd, c, TILE_CHUNKS - 1 - c) * CHUNK, CHUNK)
        sl = pl.ds(start, CHUNK)
        sls.append(sl)
        r2, v2, kk2, lw2, kd2, bt2 = (ref[sl, :] for ref in (r_ref, v_ref, kk_ref, lw_ref, kd_ref, bt_ref))
        cl = _dot_exact_lhs(tri, lw2)
        e_pos = jnp.exp(cl)
        e_neg = jnp.exp(-cl)
        al2 = -kk2 * jnp.exp(cl - lw2)
        bq2 = bt2 * e_neg
        kq2 = kd2 * e_neg
        rq2 = r2 * e_pos
        g_end2 = _scan_last(e_pos, fwd)
        for hh in range(2):
            hs = slice(hh * n, (hh + 1) * n)
            for lst, val in ((al, al2), (bq, bq2), (kq, kq2), (rq, rq2), (vv, v2), (g_end, g_end2)):
                lst.append(val[:, hs])
    items = range(2 * TILE_CHUNKS)
    quad = [_dot(jnp.concatenate([al[m], rq[m]], axis=0), jnp.concatenate([bq[m], kq[m]], axis=0), NT) for m in items]
    a_ab = [jnp.where(strict, quad[m][0:n, 0:n], 0.0) for m in items]
    a_ak = [jnp.where(strict, quad[m][0:n, n:2 * n], 0.0) for m in items]
    p_b = [jnp.where(incl, quad[m][n:2 * n, 0:n], 0.0) for m in items]
    p_k = [jnp.where(incl, quad[m][n:2 * n, n:2 * n], 0.0) for m in items]
    inv = _unit_triangular_inverses(a_ab, eye_f)
    akv = [_dot(a_ak[m], vv[m]) for m in items]
    sol = [_dot(inv[m], jnp.concatenate([al[m], akv[m]], axis=1)) for m in items]
    pbs = [_dot(p_b[m], sol[m]) for m in items]
    pkv = [_dot(p_k[m], vv[m]) for m in items]
    r_hat = [pbs[m][:, 0:n] + rq[m] for m in items]
    y0 = [pkv[m] + pbs[m][:, n:2 * n] for m in items]
    b_end = [bq[m] * g_end[m] for m in items]
    t_add = [_dot(jnp.concatenate([kq[m] * g_end[m], b_end[m]], axis=0),
                  jnp.concatenate([vv[m], sol[m][:, n:2 * n]], axis=0), TN) for m in items]
    g = [_dot(b_end[m], sol[m][:, 0:n], TN) + eye_f * g_end[m] for m in items]
    t = [t_ref[0], t_ref[1]]
    for c in range(TILE_CHUNKS):
        ys = []
        for hh in range(2):
            m = 2 * c + hh
            ys.append(y0[m] + _dot(r_hat[m], t[hh]))
            t[hh] = _dot3(g[m], t[hh]) + t_add[m]
        y_ref[sls[c], :] = jnp.concatenate(ys, axis=1)
    t_ref[0] = t[0]
    t_ref[1] = t[1]


def _rwkv_scan(pc, pd, n_lat_tiles, n_ctx_tiles):
    ntok = pc.shape[0]
    tm = TOKEN_TILE
    n_steps = n_lat_tiles + n_ctx_tiles
    hb = RWKV_W // 128

    def tile(ch, i):
        fwd = ch < RWKV_PAIRS
        kc = jnp.minimum(i, n_ctx_tiles - 1)
        kl = jnp.clip(i - n_ctx_tiles, 0, n_lat_tiles - 1)
        return jnp.where(i < n_ctx_tiles,
                         n_lat_tiles + jnp.where(fwd, kc, n_ctx_tiles - 1 - kc),
                         jnp.where(fwd, kl, n_lat_tiles - 1 - kl))

    com = lambda g: pl.BlockSpec((tm, 128), lambda ch, i: (tile(ch, i), g * hb + ch % RWKV_PAIRS))
    dirn = lambda g: pl.BlockSpec((None, tm, 128),
                                  lambda ch, i: (ch // RWKV_PAIRS, tile(ch, i), g * hb + ch % RWKV_PAIRS))
    return pl.pallas_call(
        _rwkv_body,
        grid=(2 * RWKV_PAIRS, n_steps),
        in_specs=[com(0), com(1), com(2), dirn(0), dirn(1), dirn(2)],
        out_specs=dirn(0),
        out_shape=jax.ShapeDtypeStruct((2, ntok, RWKV_W), F32),
        scratch_shapes=[pltpu.VMEM((2, RWKV_N, RWKV_N), F32)],
        compiler_params=_params("parallel", "arbitrary"),
        name="rwkv_scan",
    )(pc, pc, pc, pd, pd, pd)


def _s5_body(u_ref, tz_ref, fm_ref, em_ref, a1_ref, a2_ref, y_ref, f_scr, x_scr, *, n_chunks, n_lat_chunks):
    fwd = pl.program_id(0) < S5_GROUPS
    u = u_ref[...].astype(BF16)
    f_scr[...] = _dg(u, fm_ref[...], NN)
    a1 = a1_ref[...]
    a2 = a2_ref[...]

    def step(s, x):
        c = jnp.where(fwd, lax.rem(s + n_lat_chunks, n_chunks), n_chunks - 1 - s)
        x_scr[pl.ds(c, 1), :] = x
        return x * a1 + pltpu.roll(x, S5_STATE, 1) * a2 + f_scr[pl.ds(c, 1), :]

    lax.fori_loop(0, n_chunks, step, jnp.zeros((1, 2 * S5_STATE), F32))
    y_ref[...] = _dg(u, tz_ref[...], NN) + _dot(x_scr[...], em_ref[...])


def _s5_scan(u, tz, fm, em, a1, a2, n_lat_chunks):
    n_chunks = u.shape[0]
    n_groups, w = S5_GROUPS, S5_GROUP * CHUNK
    p2 = 2 * S5_STATE
    chain = lambda r, c: pl.BlockSpec((None, r, c), lambda b: (b, 0, 0))
    return pl.pallas_call(
        functools.partial(_s5_body, n_chunks=n_chunks, n_lat_chunks=n_lat_chunks),
        grid=(2 * n_groups,),
        in_specs=[
            pl.BlockSpec((n_chunks, w), lambda b: (0, b % n_groups)),
            chain(w, w), chain(w, p2), chain(p2, w), chain(1, p2), chain(1, p2),
        ],
        out_specs=pl.BlockSpec((None, n_chunks, w), lambda b: (b // n_groups, 0, b % n_groups)),
        out_shape=jax.ShapeDtypeStruct((2, n_chunks, n_groups * w), F32),
        scratch_shapes=[pltpu.VMEM((n_chunks, p2), F32), pltpu.VMEM((n_chunks, p2), F32)],
        compiler_params=_params("parallel"),
        name="s5_scan",
    )(u, tz, fm, em, a1, a2)


def _side_body(x_ref, zgg_ref, zu_ref, zg0_ref, zg1_ref, zg2_ref, olf_ref, olb_ref, ocf_ref, ocb_ref,
               yf_ref, yb_ref, r_ref, v_ref, ex_ref, sf_ref, sb_ref, mod_ref, gng_ref, lnw_ref, lnb_ref,
               rk_ref, g2_ref, s5d_ref, wglu_ref, wbr_ref, wout_ref, n2g_ref, bda_ref, bdo_ref,
               xo_ref, hl_ref, *, n_lat_tiles):
    i = pl.program_id(0)
    o = jnp.where(i >= n_lat_tiles, ocf_ref[...] + ocb_ref[...], olf_ref[...] + olb_ref[...])
    og = jnp.concatenate(
        [_rms(o[:, h * GLA_DV:(h + 1) * GLA_DV]) * gng_ref[...] for h in range(GLA_HEADS)], axis=1)
    zgg = zgg_ref[...]
    br_gla = og * (zgg * _sigmoid(zgg))
    y = yf_ref[...] + yb_ref[...]
    bda = bda_ref[...]
    mu = _dot_exact_rhs(y, bda)
    dy = y - mu
    var = _dot_exact_rhs(dy * dy, bda)
    yn = dy * lax.rsqrt(var + RWKV_GN_EPS) * lnw_ref[...] + lnb_ref[...]
    bonus = _dot_exact_rhs(r_ref[...] * ex_ref[:, 0:512] * rk_ref[...], bdo_ref[...])
    yn = yn + bonus * v_ref[...]
    br_rwkv = yn * _dot(_sigmoid(ex_ref[:, 512:640]), g2_ref[...])
    ys = sf_ref[...] + sb_ref[...] + s5d_ref[...] * zu_ref[...]
    lg = _dot(_gelu(ys), wglu_ref[...])
    br_s5 = lg[:, 0:S5_WIDTH] * _sigmoid(lg[:, S5_WIDTH:2 * S5_WIDTH])
    m = (_sigmoid(zg0_ref[...]) * _dot(br_gla, wbr_ref[0])
         + _sigmoid(zg1_ref[...]) * _dot(br_rwkv, wbr_ref[1])
         + _sigmoid(zg2_ref[...]) * _dot(br_s5, wbr_ref[2]))
    xn = x_ref[...] + _mod_row(mod_ref, i, n_lat_tiles, 2) * _dot(m, wout_ref[...])
    xo_ref[...] = xn
    hl_ref[...] = (_rms(xn) * n2g_ref[...] * (1.0 + _mod_row(mod_ref, i, n_lat_tiles, 4))
                   + _mod_row(mod_ref, i, n_lat_tiles, 3))


def _side_out(x, z, o_lat, o_ctx, y, pc, ex, sy, mod, consts, n_lat_tiles):
    ntok = x.shape[0]
    tm = TOKEN_TILE
    tok = lambda w, j: pl.BlockSpec((tm, w), lambda i: (i, j))
    dirn = lambda d: pl.BlockSpec((None, tm, 512), lambda i: (d, i, 0))
    lato = lambda d: pl.BlockSpec((None, tm, 512), lambda i: (d, jnp.minimum(i, n_lat_tiles - 1), 0))
    ctxo = lambda d: pl.BlockSpec((None, tm, 512), lambda i: (d, jnp.maximum(i - n_lat_tiles, 0), 0))
    return pl.pallas_call(
        functools.partial(_side_body, n_lat_tiles=n_lat_tiles),
        grid=(ntok // tm,),
        in_specs=[
            tok(D_MODEL, 0),
            tok(512, Z_GG // 512), tok(512, Z_S5 // 512),
            tok(1024, Z_GATE // 1024), tok(1024, Z_GATE // 1024 + 1), tok(1024, Z_GATE // 1024 + 2),
            lato(0), lato(1), ctxo(0), ctxo(1), dirn(0), dirn(1),
            tok(512, 0), tok(512, 1), tok(640, 0), dirn(0), dirn(1),
            _full(mod.shape),
        ] + [_full(c.shape) for c in consts],
        out_specs=[tok(D_MODEL, 0), tok(D_MODEL, 0)],
        out_shape=[jax.ShapeDtypeStruct((ntok, D_MODEL), F32)] * 2,
        compiler_params=_params("parallel"),
        name="side_out",
    )(x, z, z, z, z, z, o_lat, o_lat, o_ctx, o_ctx, y, y, pc, pc, ex, sy, sy, mod, *consts)


def _top_values(xs, k):
    xs = list(xs)
    outs = [[] for _ in xs]
    for _ in range(k):
        ms = [jnp.max(x, axis=0, keepdims=True) for x in xs]
        for o, m in zip(outs, ms):
            o.append(m)
        xs = [jnp.where(x == m, -jnp.inf, x) for x, m in zip(xs, ms)]
    return outs


PEER_HEAD_GROUP = 2


def _peer_score_body(h_ref, wh_ref, wl_ref, keys_ref, s1_ref, s2_ref, e1_ref, e2_ref, tau_ref):
    hh, hl = _split2(h_ref[...])
    for h0 in range(0, PEER_HEADS, PEER_HEAD_GROUP):
        group = range(h0, h0 + PEER_HEAD_GROUP)
        scores = []
        for hd in group:
            for s in range(2):
                rows = slice((hd * 2 + s) * PEER_HALF, (hd * 2 + s + 1) * PEER_HALF)
                wh, wl = wh_ref[rows, :], wl_ref[rows, :]
                q_t = _dg(wh, hh, NT) + (_dg(wh, hl, NT) + _dg(wl, hh, NT))
                scores.append(_dot3(keys_ref[hd, s], q_t))
        tops = _top_values(scores, PEER_TOPK)
        cands = []
        for n in range(PEER_HEAD_GROUP):
            t1, t2 = tops[2 * n], tops[2 * n + 1]
            t2_all = jnp.concatenate(t2, axis=0)
            cands.append(jnp.concatenate([t1[a] + t2_all for a in range(PEER_TOPK)], axis=0))
        bests = _top_values(cands, PEER_TOPK)
        for n, hd in enumerate(group):
            best = bests[n]
            zsum = jnp.ones_like(best[0])
            for b in best[1:]:
                zsum = zsum + jnp.exp(b - best[0])
            tau_ref[hd] = best[-1]
            s1_ref[hd] = scores[2 * n]
            s2_ref[hd] = scores[2 * n + 1]
            e1_ref[hd] = jnp.exp(scores[2 * n] - tops[2 * n][0])
            e2_ref[hd] = jnp.exp(scores[2 * n + 1] - tops[2 * n + 1][0]) / zsum


def _peer_scores(h, wq_hi, wq_lo, keys):
    ntok = h.shape[0]
    tm = TOKEN_TILE
    big = pl.BlockSpec((PEER_HEADS, PEER_NKEYS, tm), lambda i: (0, 0, i))
    big_shape = jax.ShapeDtypeStruct((PEER_HEADS, PEER_NKEYS, ntok), F32)
    return pl.pallas_call(
        _peer_score_body,
        grid=(ntok // tm,),
        in_specs=[pl.BlockSpec((tm, D_MODEL), lambda i: (i, 0)),
                  _full(wq_hi.shape), _full(wq_lo.shape), _full(keys.shape)],
        out_specs=[big, big, big, big, pl.BlockSpec((PEER_HEADS, 1, tm), lambda i: (0, 0, i))],
        out_shape=[big_shape] * 4 + [jax.ShapeDtypeStruct((PEER_HEADS, 1, ntok), F32)],
        compiler_params=_params("parallel"),
        name="peer_scores",
    )(h, wq_hi, wq_lo, keys)


def _peer_expert_body(h_ref, s1_ref, s2_ref, e1_ref, e2_ref, tau_ref, u_ref, vt_ref, o_ref, g_scr):
    j = pl.program_id(1)

    @pl.when(j == 0)
    def _():
        o_ref[...] = jnp.zeros_like(o_ref)

    n_i1 = PEER_EXPERT_BLOCK // PEER_NKEYS
    for il in range(n_i1):
        i1 = j * n_i1 + il
        acc = None
        for hd in range(PEER_HEADS):
            sel = (s2_ref[hd] + s1_ref[hd, pl.ds(i1, 1), :]) >= tau_ref[hd]
            w = jnp.where(sel, e2_ref[hd] * e1_ref[hd, pl.ds(i1, 1), :], 0.0)
            acc = w if acc is None else acc + w
        g_scr[il * PEER_NKEYS:(il + 1) * PEER_NKEYS, :] = acc
    act = _gelu(_dg(u_ref[...], h_ref[...], NT))
    o_ref[...] += _dg(vt_ref[...], (g_scr[...] * act).astype(BF16), NN)


def _peer_experts(h_bf, s1, s2, e1, e2, tau, u_bf, vt_bf):
    ntok = h_bf.shape[0]
    n_exp = u_bf.shape[0]
    tm, eb = TOKEN_TILE, PEER_EXPERT_BLOCK
    big = pl.BlockSpec((PEER_HEADS, PEER_NKEYS, tm), lambda i, j: (0, 0, i))
    return pl.pallas_call(
        _peer_expert_body,
        grid=(ntok // tm, n_exp // eb),
        in_specs=[
            pl.BlockSpec((tm, D_MODEL), lambda i, j: (i, 0)),
            big, big, big, big,
            pl.BlockSpec((PEER_HEADS, 1, tm), lambda i, j: (0, 0, i)),
            pl.BlockSpec((eb, D_MODEL), lambda i, j: (j, 0)),
            pl.BlockSpec((D_MODEL, eb), lambda i, j: (0, j)),
        ],
        out_specs=pl.BlockSpec((D_MODEL, tm), lambda i, j: (0, i)),
        out_shape=jax.ShapeDtypeStruct((D_MODEL, ntok), F32),
        scratch_shapes=[pltpu.VMEM((eb, tm), F32)],
        compiler_params=_params("parallel", "arbitrary"),
        name="peer_experts",
    )(h_bf, s1, s2, e1, e2, tau, u_bf, vt_bf)


def _residual_body(x_ref, f_ref, mod_ref, g_ref, o_ref, *, n_lat_tiles, final):
    xn = x_ref[...] + _mod_row(mod_ref, pl.program_id(0), n_lat_tiles, 5) * f_ref[...]
    if final:
        xn = _rms(xn) * g_ref[...]
    o_ref[...] = xn


def _residual(x, f, mod, g, n_lat_tiles, final):
    ntok = x.shape[0]
    tm = TOKEN_TILE
    tok = pl.BlockSpec((tm, D_MODEL), lambda i: (i, 0))
    return pl.pallas_call(
        functools.partial(_residual_body, n_lat_tiles=n_lat_tiles, final=final),
        grid=(ntok // tm,),
        in_specs=[tok, tok, _full(mod.shape), _full(g.shape)],
        out_specs=tok,
        out_shape=jax.ShapeDtypeStruct((ntok, D_MODEL), F32),
        compiler_params=_params("parallel"),
        name="residual",
    )(x, f, mod, g)


def _pad_rows(w, rows, offset):
    return jnp.zeros((rows, w.shape[1]), w.dtype).at[offset:offset + w.shape[0]].set(w)


def _s5_operators(lam_re, lam_im, log_step, b_re, b_im, c_re, c_im):
    n, g, p, ch = CHUNK, S5_GROUPS, S5_STATE, S5_GROUP
    dt = jnp.exp(log_step)[..., None]
    mag = jnp.exp(lam_re * dt)
    a_re, a_im = mag * jnp.cos(lam_im * dt), mag * jnp.sin(lam_im * dt)
    den = lam_re * lam_re + lam_im * lam_im
    f_re = ((a_re - 1.0) * lam_re + a_im * lam_im) / den
    f_im = (a_im * lam_re - (a_re - 1.0) * lam_im) / den
    bb_re = f_re[..., None] * b_re - f_im[..., None] * b_im
    bb_im = f_re[..., None] * b_im + f_im[..., None] * b_re
    tau = jnp.arange(n + 1, dtype=F32)[:, None, None, None]
    pmag = jnp.exp(lam_re * dt * tau)
    p_re, p_im = pmag * jnp.cos(lam_im * dt * tau), pmag * jnp.sin(lam_im * dt * tau)
    cp_re = c_re[None, None] * p_re[:, :, :, None, :] - c_im[None, None] * p_im[:, :, :, None, :]
    cp_im = c_re[None, None] * p_im[:, :, :, None, :] + c_im[None, None] * p_re[:, :, :, None, :]
    kern = (jnp.einsum('tdgcp,dgpe->dgect', cp_re[:n], bb_re)
            - jnp.einsum('tdgcp,dgpe->dgect', cp_im[:n], bb_im)).astype(BF16)
    zeros = jnp.zeros_like(kern)
    pos = jnp.arange(n)
    tz, fm, em = [], [], []
    for d in range(2):
        if d == 0:
            padded = jnp.concatenate([zeros[d], kern[d]], axis=-1)
            rows = [padded[..., n - j:2 * n - j] for j in range(n)]
        else:
            padded = jnp.concatenate([kern[d, ..., ::-1], zeros[d]], axis=-1)
            rows = [padded[..., n - 1 - j:2 * n - 1 - j] for j in range(n)]
        t = jnp.stack(rows, axis=2)
        tz.append(t.reshape(g, ch * n, ch * n))
        pw = slice(n - 1, None, -1) if d == 0 else slice(0, n)
        pr, pi = p_re[:n][pw][:, d], p_im[:n][pw][:, d]
        f_r = pr[..., None] * bb_re[d][None] - pi[..., None] * bb_im[d][None]
        f_i = pr[..., None] * bb_im[d][None] + pi[..., None] * bb_re[d][None]
        f = jnp.concatenate([f_r, f_i], axis=2)
        fm.append(f.transpose(1, 3, 0, 2).reshape(g, ch * n, 2 * p))
        pw = slice(0, n) if d == 0 else slice(n - 1, None, -1)
        e = jnp.concatenate([cp_re[1:][pw][:, d], -cp_im[1:][pw][:, d]], axis=3)
        em.append(e.transpose(1, 3, 2, 0).reshape(g, 2 * p, ch * n))
    ac_re = p_re[n].reshape(2 * g, 1, p)
    ac_im = p_im[n].reshape(2 * g, 1, p)
    a1 = jnp.concatenate([ac_re, ac_re], axis=2)
    a2 = jnp.concatenate([-ac_im, ac_im], axis=2)
    cat = lambda parts: jnp.concatenate(parts, axis=0).astype(BF16)
    return cat(tz), cat(fm), cat(em), a1, a2


def _block_diag(n_blocks, size, value):
    return jnp.asarray(np.kron(np.eye(n_blocks), np.full((size, size), value)), BF16)


def kernel(x, c, ctx, c_ctx, w_mod, b_mod, norm1_g, norm2_g, w_in, gla_w_a2, gla_b_a, gla_norm_g, rwkv_conv, rwkv_w0, rwkv_w2, rwkv_a0, rwkv_a2, rwkv_g2, rwkv_k_k, rwkv_k_a, rwkv_r_k, rwkv_ln_w, rwkv_ln_b, s5_lam_re, s5_lam_im, s5_log_step, s5_b_re, s5_b_im, s5_c_re, s5_c_im, s5_d, s5_w_glu, w_branch, w_out, peer_w_q, peer_sub_keys, peer_u, peer_v, final_g):
    depth = w_in.shape[0]
    n_ctx, seq = ctx.shape[1], x.shape[1]
    ntok = seq + n_ctx
    rows = seq // GRID_W
    assert x.shape[0] == 1 and seq % (GRID_W * CHUNK) == 0 and n_ctx % rows == 0
    assert seq % TOKEN_TILE == 0 and n_ctx % TOKEN_TILE == 0
    n_lat_tiles, n_ctx_tiles = seq // TOKEN_TILE, n_ctx // TOKEN_TILE
    n_chunks = ntok // CHUNK

    xs = jnp.concatenate([x[0], ctx[0]], axis=0)
    cc = jnp.zeros((8, D_MODEL), F32).at[0].set(c[0]).at[1].set(c_ctx)
    mods = _mod_vectors(cc, w_mod, b_mod)

    bd_avg = _block_diag(RWKV_HEADS, RWKV_N, 1.0 / RWKV_N)
    bd_one = _block_diag(RWKV_HEADS, RWKV_N, 1.0)

    for i in range(depth):
        mod = mods[i]
        w = w_in[i]
        zpad = lambda n: jnp.zeros((D_MODEL, n), F32)
        w_p = jnp.concatenate([
            w[:, 0:1024], w[:, 1024:1536], w[:, 3488:4000], w[:, 1568:3488], zpad(128),
            w[:, 4000:7072], w[:, 1536:1568], zpad(Z_WIDTH - Z_ALR - 32)], axis=1).astype(BF16)
        z = _in_projection(xs, mod, norm1_g[i][None], w_p, n_lat_tiles)

        conv = jnp.pad(rwkv_conv[i], ((0, 5), (0, 128)))
        w2p = jnp.stack([_pad_rows(rwkv_w2[i, d], 128, d * 64) for d in range(2)])
        a2p = jnp.stack([_pad_rows(rwkv_a2[i, d], 128, d * 64) for d in range(2)])
        gwp = jnp.stack([_pad_rows(gla_w_a2[i, d], 128, d * GLA_RANK) for d in range(2)])
        pc, pd, ex, la = _mixer_prep(z, conv, rwkv_w0[i], w2p, rwkv_a0[i], a2p, rwkv_k_k[i][None],
                                     rwkv_k_a[i][None], bd_one, gwp, gla_b_a[i], n_lat_tiles)

        o_lat, o_ctx = _gla_scan(z, la, seq, n_ctx)
        y = _rwkv_scan(pc, pd, n_lat_tiles, n_ctx_tiles)

        tz, fm, em, a1, a2 = _s5_operators(s5_lam_re[i], s5_lam_im[i], s5_log_step[i], s5_b_re[i], s5_b_im[i],
                                           s5_c_re[i], s5_c_im[i])
        su = z[:, Z_S5:Z_S5 + S5_WIDTH].reshape(n_chunks, CHUNK, S5_WIDTH).transpose(0, 2, 1)
        sy = _s5_scan(su.reshape(n_chunks, S5_WIDTH * CHUNK), tz, fm, em, a1, a2, seq // CHUNK)
        sy = sy.reshape(2, n_chunks, S5_WIDTH, CHUNK).transpose(0, 1, 3, 2).reshape(2, ntok, S5_WIDTH)

        consts = (
            gla_norm_g[i][None], rwkv_ln_w[i][None], rwkv_ln_b[i][None], rwkv_r_k[i].reshape(1, RWKV_W),
            rwkv_g2[i].astype(BF16), s5_d[i][None], s5_w_glu[i].astype(BF16), w_branch[i].astype(BF16),
            w_out[i].astype(BF16), norm2_g[i][None], bd_avg, bd_one)
        xs, hl = _side_out(xs, z, o_lat, o_ctx, y, pc, ex, sy, mod, consts, n_lat_tiles)

        wq_t = peer_w_q[i].T
        wq_hi = wq_t.astype(BF16)
        wq_lo = (wq_t - wq_hi.astype(F32)).astype(BF16)
        s1, s2, e1, e2, tau = _peer_scores(hl, wq_hi, wq_lo, peer_sub_keys[i])
        f_t = _peer_experts(hl.astype(BF16), s1, s2, e1, e2, tau, peer_u[i].astype(BF16), peer_v[i].T.astype(BF16))
        xs = _residual(xs, f_t.T, mod, final_g[None], n_lat_tiles, final=(i == depth - 1))

    return xs[:seq][None]
```
